```python
import math
import jax, jax.numpy as jnp
from jax import lax
import numpy as np

D_MODEL = 1024
BATCH = 1
SEQ = 16384
DEPTH = 1

ATTN_WIDTH = D_MODEL // 2
CONV_WIDTH = D_MODEL - ATTN_WIDTH
DA_HEAD_DIM = 64
DA_V_DIM = 2 * DA_HEAD_DIM
DA_HEADS = ATTN_WIDTH // DA_V_DIM
CONV_K = 31
FFN_CONV_K = 3
D_FF = 2816
ROPE_THETA = 10000.0
Q_BLOCK = 128
EPS = 1e-6
IN_COLS = 3 * ATTN_WIDTH + 2 * CONV_WIDTH

kernel_name = "hybrid_diffattn_conformer_convffn_sandwich"


def rms_norm(x, g):
    xf = x.astype(jnp.float32)
    y = xf * lax.rsqrt(jnp.mean(xf * xf, axis=-1, keepdims=True) + EPS)
    return (y * g.astype(jnp.float32)).astype(x.dtype)


def layer_norm(x, g, b):
    xf = x.astype(jnp.float32)
    mu = jnp.mean(xf, axis=-1, keepdims=True)
    var = jnp.mean(jnp.square(xf - mu), axis=-1, keepdims=True)
    y = (xf - mu) * lax.rsqrt(var + EPS)
    return (y * g.astype(jnp.float32) + b.astype(jnp.float32)).astype(x.dtype)


def causal_dwconv(x, w, b):
    k, c = w.shape
    y = lax.conv_general_dilated(
        x, w[:, None, :].astype(x.dtype), window_strides=(1,), padding=((k - 1, 0),),
        dimension_numbers=("NWC", "WIO", "NWC"), feature_group_count=c)
    return y + b.astype(x.dtype)


def rope(t, positions):
    d = t.shape[-1]
    inv_freq = ROPE_THETA ** (-jnp.arange(0, d, 2, dtype=jnp.float32) / d)
    ang = positions.astype(jnp.float32)[:, :, None] * inv_freq
    cos = jnp.cos(ang)[:, :, None, None, :]
    sin = jnp.sin(ang)[:, :, None, None, :]
    tf = t.astype(jnp.float32)
    t1, t2 = jnp.split(tf, 2, axis=-1)
    out = jnp.concatenate([t1 * cos - t2 * sin, t2 * cos + t1 * sin], axis=-1)
    return out.astype(t.dtype)


def diff_attention(q, k, v, lam, positions):
    bsz, s, h, _, d = q.shape
    e = v.shape[-1]
    nblk = s // Q_BLOCK
    scale = d ** -0.5
    qb = q.reshape(bsz, nblk, Q_BLOCK, h, 2, d).transpose(1, 0, 2, 3, 4, 5)
    pb = positions.reshape(bsz, nblk, Q_BLOCK).transpose(1, 0, 2)
    neg = jnp.finfo(jnp.float32).min

    def one_block(args):
        qi, pi = args
        sc = jnp.einsum("bqhcd,bkhcd->bhcqk", qi, k,
                        preferred_element_type=jnp.float32) * scale
        mask = pi[:, :, None] >= positions[:, None, :]
        sc = jnp.where(mask[:, None, None], sc, neg)
        p = jax.nn.softmax(sc, axis=-1)
        a = p[:, :, 0] - lam * p[:, :, 1]
        return jnp.einsum("bhqk,bkhe->bqhe", a.astype(v.dtype), v)

    out = lax.map(one_block, (qb, pb))
    return out.transpose(1, 0, 2, 3, 4).reshape(bsz, s, h, e)


def setup_inputs(seed: int = 0) -> dict:
    key = jax.random.key(seed)
    ks = jax.random.split(key, 24)
    f32 = jnp.float32
    L = DEPTH

    def nrm(k, shape, scale):
        return jax.random.normal(k, shape, f32) * scale

    def gain(k, shape):
        return 1.0 + 0.05 * jax.random.normal(k, shape, f32)

    x = jax.random.normal(ks[0], (BATCH, SEQ, D_MODEL), f32)
    positions = jnp.broadcast_to(jnp.arange(SEQ, dtype=jnp.int32)[None, :], (BATCH, SEQ))
    return {
        "x": x,
        "positions": positions,
        "attn_pre_g": gain(ks[1], (L, D_MODEL)),
        "attn_post_g": gain(ks[2], (L, D_MODEL)),
        "w_in": nrm(ks[3], (L, D_MODEL, IN_COLS), D_MODEL ** -0.5),
        "lambda_q1": nrm(ks[4], (L, DA_HEAD_DIM), 0.1),
        "lambda_k1": nrm(ks[5], (L, DA_HEAD_DIM), 0.1),
        "lambda_q2": nrm(ks[6], (L, DA_HEAD_DIM), 0.1),
        "lambda_k2": nrm(ks[7], (L, DA_HEAD_DIM), 0.1),
        "subln_g": gain(ks[8], (L, DA_V_DIM)),
        "conv_dw_w": nrm(ks[9], (L, CONV_K, CONV_WIDTH), CONV_K ** -0.5),
        "conv_dw_b": nrm(ks[10], (L, CONV_WIDTH), 0.02),
        "conv_ln_g": gain(ks[11], (L, CONV_WIDTH)),
        "conv_ln_b": nrm(ks[12], (L, CONV_WIDTH), 0.02),
        "w_out": nrm(ks[13], (L, ATTN_WIDTH + CONV_WIDTH, D_MODEL), (ATTN_WIDTH + CONV_WIDTH) ** -0.5),
        "ffn_pre_g": gain(ks[14], (L, D_MODEL)),
        "ffn_post_g": gain(ks[15], (L, D_MODEL)),
        "w_up": nrm(ks[16], (L, D_MODEL, 2 * D_FF), D_MODEL ** -0.5),
        "ffn_dw_w": nrm(ks[17], (L, FFN_CONV_K, 2 * D_FF), FFN_CONV_K ** -0.5),
        "ffn_dw_b": nrm(ks[18], (L, 2 * D_FF), 0.02),
        "w_down": nrm(ks[19], (L, D_FF, D_MODEL), D_FF ** -0.5),
    }


def reference(x, positions, attn_pre_g, attn_post_g, w_in, lambda_q1, lambda_k1,
              lambda_q2, lambda_k2, subln_g, conv_dw_w, conv_dw_b, conv_ln_g,
              conv_ln_b, w_out, ffn_pre_g, ffn_post_g, w_up, ffn_dw_w, ffn_dw_b,
              w_down):
    bsz, s, _ = x.shape
    for l in range(DEPTH):
        h = rms_norm(x, attn_pre_g[l])
        proj = h @ w_in[l]
        q, k, v, cg = jnp.split(
            proj, [ATTN_WIDTH, 2 * ATTN_WIDTH, 3 * ATTN_WIDTH], axis=-1)

        q = rope(q.reshape(bsz, s, DA_HEADS, 2, DA_HEAD_DIM), positions)
        k = rope(k.reshape(bsz, s, DA_HEADS, 2, DA_HEAD_DIM), positions)
        v = v.reshape(bsz, s, DA_HEADS, DA_V_DIM)
        lambda_init = 0.8 - 0.6 * math.exp(-0.3 * l)
        lam = (jnp.exp(jnp.sum(lambda_q1[l].astype(jnp.float32) * lambda_k1[l].astype(jnp.float32)))
               - jnp.exp(jnp.sum(lambda_q2[l].astype(jnp.float32) * lambda_k2[l].astype(jnp.float32)))
               + lambda_init)
        att = diff_attention(q, k, v, lam, positions)
        att = rms_norm(att, subln_g[l]) * (1.0 - lambda_init)
        att = att.reshape(bsz, s, ATTN_WIDTH)

        ga, gb = jnp.split(cg, 2, axis=-1)
        c = ga * jax.nn.sigmoid(gb)
        c = causal_dwconv(c, conv_dw_w[l], conv_dw_b[l])
        c = jax.nn.silu(layer_norm(c, conv_ln_g[l], conv_ln_b[l]))

        mix = jnp.concatenate([att, c], axis=-1) @ w_out[l]
        x = x + rms_norm(mix, attn_post_g[l])

        h = rms_norm(x, ffn_pre_g[l])
        u = causal_dwconv(h @ w_up[l], ffn_dw_w[l], ffn_dw_b[l])
        gate, up = jnp.split(u, 2, axis=-1)
        y = (jax.nn.gelu(gate, approximate=True) * up) @ w_down[l]
        x = x + rms_norm(y, ffn_post_g[l])
    return x
```

```python
import functools
import math

import jax
import jax.numpy as jnp
from jax import lax
from jax.experimental import pallas as pl
from jax.experimental.pallas import tpu as pltpu

D_MODEL = 1024
ATTN_WIDTH = 512
CONV_WIDTH = 512
HEAD_DIM = 64
V_DIM = 2 * HEAD_DIM
N_HEADS = ATTN_WIDTH // V_DIM
CONV_K = 31
FFN_CONV_K = 3
D_FF = 2816
ROPE_THETA = 10000.0
EPS = 1e-6

LANES = 128
SUBLANES = 8
MXU_COLS = 256
VMEM_LIMIT_BYTES = 56 * 1024 * 1024

ROW_TILE = 512
Q_TILE = 256
KV_TILE = 512
CONV_HALO = 32
CONV_ROWS = 64
FFN_HALO = 8
FF_CHUNK = 256

MASK_VALUE = -1e30

_BF16 = jnp.bfloat16
_F32 = jnp.float32


def _rms_scale(x):
    return lax.rsqrt(jnp.mean(x * x, axis=-1, keepdims=True) + EPS)


def _nt_dot(a, b):
    return lax.dot_general(a, b, (((1,), (1,)), ((), ())), preferred_element_type=_F32)


def _in_proj_kernel(x_ref, pos_ref, g_ref, invf_ref, wq_ref, wk_ref, wvt_ref, wcg_ref,
                    q_ref, k_ref, vt_ref, c_ref):
    x = x_ref[...]
    h = (x * _rms_scale(x) * g_ref[...]).astype(_BF16)

    ang = pos_ref[...].astype(_F32) * invf_ref[...]
    cos = jnp.cos(ang)
    sin = jnp.sin(ang)
    lane = lax.broadcasted_iota(jnp.int32, ang.shape, 1)
    first_half = (lane & (HEAD_DIM // 2)) == 0
    sin_first = jnp.where(first_half, -sin, 0.0)
    sin_second = jnp.where(first_half, 0.0, sin)

    def rope(t):
        from_right = pltpu.roll(t, LANES - HEAD_DIM // 2, 1)
        from_left = pltpu.roll(t, HEAD_DIM // 2, 1)
        return t * cos + from_right * sin_first + from_left * sin_second

    scale = HEAD_DIM ** -0.5
    q = jnp.dot(h, wq_ref[...], preferred_element_type=_F32)
    k = jnp.dot(h, wk_ref[...], preferred_element_type=_F32)
    for j in range(N_HEADS):
        sl = slice(j * LANES, (j + 1) * LANES)
        q_ref[:, sl] = (rope(q[:, sl]) * scale).astype(_BF16)
        k_ref[:, sl] = rope(k[:, sl]).astype(_BF16)

    vt = _nt_dot(wvt_ref[...], h)
    for j in range(N_HEADS):
        vt_ref[j, 0] = vt[j * V_DIM:(j + 1) * V_DIM, :].astype(_BF16)

    cg = jnp.dot(h, wcg_ref[...], preferred_element_type=_F32)
    c_ref[...] = cg[:, :CONV_WIDTH] * jax.nn.sigmoid(cg[:, CONV_WIDTH:])


def _in_proj(x, pos_col, g, invf, wq, wk, wvt, wcg):
    s = x.shape[0]
    n_tiles = s // ROW_TILE
    const = lambda i: (0, 0)
    return pl.pallas_call(
        _in_proj_kernel,
        grid=(n_tiles,),
        in_specs=[
            pl.BlockSpec((ROW_TILE, D_MODEL), lambda i: (i, 0)),
            pl.BlockSpec((ROW_TILE, 1), lambda i: (i, 0)),
            pl.BlockSpec((1, D_MODEL), const),
            pl.BlockSpec((1, LANES), const),
            pl.BlockSpec((D_MODEL, ATTN_WIDTH), const),
            pl.BlockSpec((D_MODEL, ATTN_WIDTH), const),
            pl.BlockSpec((ATTN_WIDTH, D_MODEL), const),
            pl.BlockSpec((D_MODEL, 2 * CONV_WIDTH), const),
        ],
        out_specs=[
            pl.BlockSpec((ROW_TILE, ATTN_WIDTH), lambda i: (i, 0)),
            pl.BlockSpec((ROW_TILE, ATTN_WIDTH), lambda i: (i, 0)),
            pl.BlockSpec((N_HEADS, 1, V_DIM, ROW_TILE), lambda i: (0, i, 0, 0)),
            pl.BlockSpec((ROW_TILE, CONV_WIDTH), lambda i: (i, 0)),
        ],
        out_shape=[
            jax.ShapeDtypeStruct((s, ATTN_WIDTH), _BF16),
            jax.ShapeDtypeStruct((s, ATTN_WIDTH), _BF16),
            jax.ShapeDtypeStruct((N_HEADS, n_tiles, V_DIM, ROW_TILE), _BF16),
            jax.ShapeDtypeStruct((s, CONV_WIDTH), _F32),
        ],
        compiler_params=pltpu.CompilerParams(
            dimension_semantics=("arbitrary",), vmem_limit_bytes=VMEM_LIMIT_BYTES),
        name="in_proj",
    )(x, pos_col, g, invf, wq, wk, wvt, wcg)


def _attn_kernel(qmin_ref, qmax_ref, kmin_ref, kmax_ref,
                 q_ref, k_ref, vt_ref, posq_ref, posk_ref,
                 lq1_ref, lk1_ref, lq2_ref, lk2_ref, g_ref,
                 o_ref,
                 qz_ref, m_ref, l_ref, acc_ref, *, lambda_init):
    qi = pl.program_id(1)
    n_kv = posk_ref.shape[0]
    tq = q_ref.shape[0]

    q = q_ref[...]
    lane = lax.broadcasted_iota(jnp.int32, q.shape, 1)
    zero = jnp.zeros_like(q)
    qz_ref[0] = jnp.where(lane < HEAD_DIM, q, zero)
    qz_ref[1] = jnp.where(lane < HEAD_DIM, zero, q)
    m_ref[...] = jnp.full(m_ref.shape, MASK_VALUE, _F32)
    l_ref[...] = jnp.zeros(l_ref.shape, _F32)
    acc_ref[...] = jnp.zeros(acc_ref.shape, _F32)

    q_lo = qmin_ref[qi]
    q_hi = qmax_ref[qi]

    def step(kj, masked):
        kb = k_ref[pl.ds(pl.multiple_of(kj * KV_TILE, KV_TILE), KV_TILE), :]
        vb = vt_ref[0, kj]
        if masked:
            pk = jnp.broadcast_to(posk_ref[kj], (LANES, KV_TILE))
            pk = jnp.tile(pk.T, (1, tq // LANES))
            visible = posq_ref[...] >= pk
        for c in range(2):
            s = _nt_dot(kb, qz_ref[c])
            if masked:
                s = jnp.where(visible, s, MASK_VALUE)
            m_prev = m_ref[c]
            m_new = jnp.maximum(m_prev, jnp.max(s, axis=0, keepdims=True))
            alpha = jnp.exp(m_prev - m_new)
            p = jnp.exp(s - m_new)
            l_ref[c] = alpha * l_ref[c] + jnp.sum(p, axis=0, keepdims=True)
            acc_ref[c] = alpha * acc_ref[c] + jnp.dot(
                vb, p.astype(_BF16), preferred_element_type=_F32)
            m_ref[c] = m_new

    def body(kj, carry):
        k_lo = kmin_ref[kj]
        k_hi = kmax_ref[kj]
        needed = k_lo <= q_hi
        full = k_hi <= q_lo

        @pl.when(jnp.logical_and(needed, full))
        def _():
            step(kj, False)

        @pl.when(jnp.logical_and(needed, jnp.logical_not(full)))
        def _():
            step(kj, True)

        return carry

    lax.fori_loop(0, n_kv, body, 0)

    lam = (jnp.exp(jnp.sum(lq1_ref[...] * lk1_ref[...], keepdims=True))
           - jnp.exp(jnp.sum(lq2_ref[...] * lk2_ref[...], keepdims=True))
           + lambda_init)
    o = acc_ref[0] / l_ref[0] - lam * (acc_ref[1] / l_ref[1])
    o = o * lax.rsqrt(jnp.mean(o * o, axis=0, keepdims=True) + EPS)
    o = (o * g_ref[...]) * (1.0 - lambda_init)
    o_ref[...] = o.T.astype(o_ref.dtype)


def _attention(q, k, vt, pos_row, lq1, lk1, lq2, lk2, g_col, lambda_init):
    s = q.shape[0]
    n_q = s // Q_TILE
    n_kv = s // KV_TILE
    pos = pos_row[0]
    qmin = pos.reshape(n_q, Q_TILE).min(axis=1)
    qmax = pos.reshape(n_q, Q_TILE).max(axis=1)
    kmin = pos.reshape(n_kv, KV_TILE).min(axis=1)
    kmax = pos.reshape(n_kv, KV_TILE).max(axis=1)
    posk = pos_row.reshape(n_kv, 1, KV_TILE)
    small = lambda h, i, *_: (0, 0)
    grid_spec = pltpu.PrefetchScalarGridSpec(
        num_scalar_prefetch=4,
        grid=(N_HEADS, n_q),
        in_specs=[
            pl.BlockSpec((Q_TILE, V_DIM), lambda h, i, *_: (i, h)),
            pl.BlockSpec((s, V_DIM), lambda h, i, *_: (0, h)),
            pl.BlockSpec((1, n_kv, V_DIM, KV_TILE), lambda h, i, *_: (h, 0, 0, 0)),
            pl.BlockSpec((1, Q_TILE), lambda h, i, *_: (0, i)),
            pl.BlockSpec((n_kv, 1, KV_TILE), lambda h, i, *_: (0, 0, 0)),
            pl.BlockSpec((1, HEAD_DIM), small),
            pl.BlockSpec((1, HEAD_DIM), small),
            pl.BlockSpec((1, HEAD_DIM), small),
            pl.BlockSpec((1, HEAD_DIM), small),
            pl.BlockSpec((V_DIM, 1), small),
        ],
        out_specs=pl.BlockSpec((Q_TILE, V_DIM), lambda h, i, *_: (i, h)),
        scratch_shapes=[
            pltpu.VMEM((2, Q_TILE, V_DIM), _BF16),
            pltpu.VMEM((2, 1, Q_TILE), _F32),
            pltpu.VMEM((2, 1, Q_TILE), _F32),
            pltpu.VMEM((2, V_DIM, Q_TILE), _F32),
        ],
    )
    return pl.pallas_call(
        functools.partial(_attn_kernel, lambda_init=lambda_init),
        grid_spec=grid_spec,
        out_shape=jax.ShapeDtypeStruct((s, ATTN_WIDTH), _BF16),
        compiler_params=pltpu.CompilerParams(
            dimension_semantics=("arbitrary", "arbitrary"), vmem_limit_bytes=VMEM_LIMIT_BYTES),
        name="diff_attn",
    )(qmin, qmax, kmin, kmax, q, k, vt, pos_row, posk, lq1, lk1, lq2, lk2, g_col)


def _mix_kernel(c_ref, chalo_ref, att_ref, x_ref, dww_ref, dwb_ref, lng_ref, lnb_ref,
                wo_ref, g_ref, o_ref, ext_ref, cact_ref):
    i = pl.program_id(0)
    tm = c_ref.shape[0]
    ext_ref[0:CONV_HALO] = jnp.where(i > 0, chalo_ref[...], 0.0)
    ext_ref[CONV_HALO:CONV_HALO + tm] = c_ref[...]

    bias = dwb_ref[...]
    ln_g = lng_ref[...]
    ln_b = lnb_ref[...]
    first_tap = CONV_HALO - (CONV_K - 1)

    for r0 in range(0, tm, CONV_ROWS):
        acc = jnp.broadcast_to(bias, (CONV_ROWS, CONV_WIDTH))
        for j in range(CONV_K):
            start = r0 + first_tap + j
            acc = acc + dww_ref[j:j + 1, :] * ext_ref[start:start + CONV_ROWS, :]
        mu = jnp.mean(acc, axis=-1, keepdims=True)
        d = acc - mu
        var = jnp.mean(d * d, axis=-1, keepdims=True)
        y = d * lax.rsqrt(var + EPS) * ln_g + ln_b
        cact_ref[r0:r0 + CONV_ROWS, :] = (y * jax.nn.sigmoid(y)).astype(cact_ref.dtype)

    mix = (jnp.dot(att_ref[...], wo_ref[0:ATTN_WIDTH, :], preferred_element_type=_F32)
           + jnp.dot(cact_ref[...], wo_ref[ATTN_WIDTH:, :], preferred_element_type=_F32))
    o_ref[...] = x_ref[...] + mix * _rms_scale(mix) * g_ref[...]


def _mix(c, att, x, dww, dwb, lng, lnb, wo, g):
    s = x.shape[0]
    n_tiles = s // ROW_TILE
    halo_blocks = ROW_TILE // CONV_HALO
    const = lambda i: (0, 0)
    row = lambda i: (i, 0)
    return pl.pallas_call(
        _mix_kernel,
        grid=(n_tiles,),
        in_specs=[
            pl.BlockSpec((ROW_TILE, CONV_WIDTH), row),
            pl.BlockSpec((CONV_HALO, CONV_WIDTH), lambda i: (jnp.maximum(i * halo_blocks - 1, 0), 0)),
            pl.BlockSpec((ROW_TILE, ATTN_WIDTH), row),
            pl.BlockSpec((ROW_TILE, D_MODEL), row),
            pl.BlockSpec((CONV_K, CONV_WIDTH), const),
            pl.BlockSpec((1, CONV_WIDTH), const),
            pl.BlockSpec((1, CONV_WIDTH), const),
            pl.BlockSpec((1, CONV_WIDTH), const),
            pl.BlockSpec((D_MODEL, D_MODEL), const),
            pl.BlockSpec((1, D_MODEL), const),
        ],
        out_specs=pl.BlockSpec((ROW_TILE, D_MODEL), row),
        out_shape=jax.ShapeDtypeStruct((s, D_MODEL), _F32),
        scratch_shapes=[
            pltpu.VMEM((CONV_HALO + ROW_TILE, CONV_WIDTH), _F32),
            pltpu.VMEM((ROW_TILE, CONV_WIDTH), _BF16),
        ],
        compiler_params=pltpu.CompilerParams(
            dimension_semantics=("arbitrary",), vmem_limit_bytes=VMEM_LIMIT_BYTES),
        name="conv_mix",
    )(c, c, att, x, dww, dwb, lng, lnb, wo, g)


def _ffn_kernel(x_ref, xhalo_ref, pre_g_ref, wup_ref, dww_ref, dwb_ref, wdn_ref, post_g_ref,
                o_ref, ug_ref, uu_ref, act_ref):
    i = pl.program_id(0)
    tm = x_ref.shape[0]
    x = x_ref[...]
    xe = jnp.concatenate([xhalo_ref[...], x], axis=0)
    h = (xe * _rms_scale(xe) * pre_g_ref[...]).astype(_BF16)

    def conv(u_ref, col):
        w = dww_ref[:, col:col + FF_CHUNK]
        acc = dwb_ref[:, col:col + FF_CHUNK] + w[2:3] * u_ref[FFN_HALO:FFN_HALO + tm, :]
        acc = acc + w[1:2] * u_ref[FFN_HALO - 1:FFN_HALO - 1 + tm, :]
        return acc + w[0:1] * u_ref[FFN_HALO - 2:FFN_HALO - 2 + tm, :]

    def store_u(u_ref, u):
        u_ref[0:FFN_HALO] = jnp.where(i > 0, u[0:FFN_HALO], 0.0)
        u_ref[FFN_HALO:] = u[FFN_HALO:]

    for c in range(D_FF // FF_CHUNK):
        gcol = c * FF_CHUNK
        ucol = D_FF + c * FF_CHUNK
        store_u(ug_ref, jnp.dot(h, wup_ref[:, gcol:gcol + FF_CHUNK], preferred_element_type=_F32))
        store_u(uu_ref, jnp.dot(h, wup_ref[:, ucol:ucol + FF_CHUNK], preferred_element_type=_F32))
        gate = conv(ug_ref, gcol)
        up = conv(uu_ref, ucol)
        act_ref[:, gcol:gcol + FF_CHUNK] = (jax.nn.gelu(gate, approximate=True) * up).astype(_BF16)

    y = jnp.dot(act_ref[...], wdn_ref[...], preferred_element_type=_F32)
    o_ref[...] = x + y * _rms_scale(y) * post_g_ref[...]


def _ffn(x, pre_g, wup, dww, dwb, wdn, post_g):
    s = x.shape[0]
    n_tiles = s // ROW_TILE
    halo_blocks = ROW_TILE // FFN_HALO
    const = lambda i: (0, 0)
    row = lambda i: (i, 0)
    resident = dict(pipeline_mode=pl.Buffered(1))
    return pl.pallas_call(
        _ffn_kernel,
        grid=(n_tiles,),
        in_specs=[
            pl.BlockSpec((ROW_TILE, D_MODEL), row),
            pl.BlockSpec((FFN_HALO, D_MODEL), lambda i: (jnp.maximum(i * halo_blocks - 1, 0), 0)),
            pl.BlockSpec((1, D_MODEL), const),
            pl.BlockSpec((D_MODEL, 2 * D_FF), const, **resident),
            pl.BlockSpec((FFN_CONV_K, 2 * D_FF), const),
            pl.BlockSpec((1, 2 * D_FF), const),
            pl.BlockSpec((D_FF, D_MODEL), const, **resident),
            pl.BlockSpec((1, D_MODEL), const),
        ],
        out_specs=pl.BlockSpec((ROW_TILE, D_MODEL), row),
        out_shape=jax.ShapeDtypeStruct((s, D_MODEL), _F32),
        scratch_shapes=[
            pltpu.VMEM((FFN_HALO + ROW_TILE, FF_CHUNK), _F32),
            pltpu.VMEM((FFN_HALO + ROW_TILE, FF_CHUNK), _F32),
            pltpu.VMEM((ROW_TILE, D_FF), _BF16),
        ],
        compiler_params=pltpu.CompilerParams(
            dimension_semantics=("arbitrary",), vmem_limit_bytes=VMEM_LIMIT_BYTES),
        name="conv_ffn",
    )(x, x, pre_g, wup, dww, dwb, wdn, post_g)


def kernel(x, positions, attn_pre_g, attn_post_g, w_in, lambda_q1, lambda_k1, lambda_q2, lambda_k2,
           subln_g, conv_dw_w, conv_dw_b, conv_ln_g, conv_ln_b, w_out, ffn_pre_g, ffn_post_g, w_up,
           ffn_dw_w, ffn_dw_b, w_down):
    bsz, s, d = x.shape
    depth = w_in.shape[0]
    assert bsz == 1 and d == D_MODEL
    assert s % ROW_TILE == 0 and s % KV_TILE == 0 and s % Q_TILE == 0 and ROW_TILE == KV_TILE

    pos_row = positions.reshape(1, s).astype(jnp.int32)
    pos_col = pos_row.reshape(s, 1)
    half = HEAD_DIM // 2
    inv_freq = ROPE_THETA ** (-jnp.arange(0, HEAD_DIM, 2, dtype=_F32) / HEAD_DIM)
    invf = jnp.tile(inv_freq, LANES // half).reshape(1, LANES)

    row = lambda v: v.reshape(1, -1).astype(_F32)
    xs = x[0]
    for l in range(depth):
        lambda_init = 0.8 - 0.6 * math.exp(-0.3 * l)
        w = w_in[l].astype(_BF16)
        wq = w[:, :ATTN_WIDTH]
        wk = w[:, ATTN_WIDTH:2 * ATTN_WIDTH]
        wvt = w[:, 2 * ATTN_WIDTH:3 * ATTN_WIDTH].T
        wcg = w[:, 3 * ATTN_WIDTH:]
        q, k, vt, c = _in_proj(xs, pos_col, row(attn_pre_g[l]), invf, wq, wk, wvt, wcg)
        att = _attention(q, k, vt, pos_row, row(lambda_q1[l]), row(lambda_k1[l]),
                         row(lambda_q2[l]), row(lambda_k2[l]),
                         subln_g[l].reshape(V_DIM, 1).astype(_F32), lambda_init)
        xs = _mix(c, att, xs, conv_dw_w[l].astype(_F32), row(conv_dw_b[l]), row(conv_ln_g[l]),
                  row(conv_ln_b[l]), w_out[l].astype(_BF16), row(attn_post_g[l]))
        xs = _ffn(xs, row(ffn_pre_g[l]), w_up[l].astype(_BF16), ffn_dw_w[l].astype(_F32),
                  row(ffn_dw_b[l]), w_down[l].astype(_BF16), row(ffn_post_g[l]))
    return xs[None]
```

```python
import functools
import math

import jax
import jax.numpy as jnp
from jax import lax
from jax.experimental import pallas as pl
from jax.experimental.pallas import tpu as pltpu

D_MODEL = 1024
ATTN_WIDTH = 512
CONV_WIDTH = 512
HEAD_DIM = 64
V_DIM = 2 * HEAD_DIM
N_HEADS = ATTN_WIDTH // V_DIM
CONV_K = 31
FFN_CONV_K = 3
D_FF = 2816
ROPE_THETA = 10000.0
EPS = 1e-6

LANES = 128
SUBLANES = 8
MXU_COLS = 256
VMEM_LIMIT_BYTES = 56 * 1024 * 1024

ROW_TILE = 512
Q_TILE = 256
KV_TILE = 512
CONV_HALO = 32
CONV_ROWS = 64
FFN_HALO = 8
FF_CHUNK = 256

MASK_VALUE = -1e30

_BF16 = jnp.bfloat16
_F32 = jnp.float32


def _rms_scale(x):
    return lax.rsqrt(jnp.mean(x * x, axis=-1, keepdims=True) + EPS)


def _nt_dot(a, b):
    return lax.dot_general(a, b, (((1,), (1,)), ((), ())), preferred_element_type=_F32)


def _in_proj_kernel(x_ref, pos_ref, g_ref, invf_ref, wq_ref, wk_ref, wvt_ref, wcg_ref,
                    q_ref, k_ref, vt_ref, c_ref):
    x = x_ref[...]
    h = (x * _rms_scale(x) * g_ref[...]).astype(_BF16)

    ang = pos_ref[...].astype(_F32) * invf_ref[...]
    cos = jnp.cos(ang)
    sin = jnp.sin(ang)
    lane = lax.broadcasted_iota(jnp.int32, ang.shape, 1)
    first_half = (lane & (HEAD_DIM // 2)) == 0
    sin_first = jnp.where(first_half, -sin, 0.0)
    sin_second = jnp.where(first_half, 0.0, sin)

    def rope(t):
        from_right = pltpu.roll(t, LANES - HEAD_DIM // 2, 1)
        from_left = pltpu.roll(t, HEAD_DIM // 2, 1)
        return t * cos + from_right * sin_first + from_left * sin_second

    scale = HEAD_DIM ** -0.5
    q = jnp.dot(h, wq_ref[...], preferred_element_type=_F32)
    k = jnp.dot(h, wk_ref[...], preferred_element_type=_F32)
    for j in range(N_HEADS):
        sl = slice(j * LANES, (j + 1) * LANES)
        q_ref[:, sl] = (rope(q[:, sl]) * scale).astype(_BF16)
        k_ref[:, sl] = rope(k[:, sl]).astype(_BF16)

    vt = _nt_dot(wvt_ref[...], h)
    for j in range(N_HEADS):
        vt_ref[j, 0] = vt[j * V_DIM:(j + 1) * V_DIM, :].astype(_BF16)

    cg = jnp.dot(h, wcg_ref[...], preferred_element_type=_F32)
    c_ref[...] = cg[:, :CONV_WIDTH] * jax.nn.sigmoid(cg[:, CONV_WIDTH:])


def _in_proj(x, pos_col, g, invf, wq, wk, wvt, wcg):
    s = x.shape[0]
    n_tiles = s // ROW_TILE
    const = lambda i: (0, 0)
    return pl.pallas_call(
        _in_proj_kernel,
        grid=(n_tiles,),
        in_specs=[
            pl.BlockSpec((ROW_TILE, D_MODEL), lambda i: (i, 0)),
            pl.BlockSpec((ROW_TILE, 1), lambda i: (i, 0)),
            pl.BlockSpec((1, D_MODEL), const),
            pl.BlockSpec((1, LANES), const),
            pl.BlockSpec((D_MODEL, ATTN_WIDTH), const),
            pl.BlockSpec((D_MODEL, ATTN_WIDTH), const),
            pl.BlockSpec((ATTN_WIDTH, D_MODEL), const),
            pl.BlockSpec((D_MODEL, 2 * CONV_WIDTH), const),
        ],
        out_specs=[
            pl.BlockSpec((ROW_TILE, ATTN_WIDTH), lambda i: (i, 0)),
            pl.BlockSpec((ROW_TILE, ATTN_WIDTH), lambda i: (i, 0)),
            pl.BlockSpec((N_HEADS, 1, V_DIM, ROW_TILE), lambda i: (0, i, 0, 0)),
            pl.BlockSpec((ROW_TILE, CONV_WIDTH), lambda i: (i, 0)),
        ],
        out_shape=[
            jax.ShapeDtypeStruct((s, ATTN_WIDTH), _BF16),
            jax.ShapeDtypeStruct((s, ATTN_WIDTH), _BF16),
            jax.ShapeDtypeStruct((N_HEADS, n_tiles, V_DIM, ROW_TILE), _BF16),
            jax.ShapeDtypeStruct((s, CONV_WIDTH), _F32),
        ],
        compiler_params=pltpu.CompilerParams(
            dimension_semantics=("arbitrary",), vmem_limit_bytes=VMEM_LIMIT_BYTES),
        name="in_proj",
    )(x, pos_col, g, invf, wq, wk, wvt, wcg)


def _attn_kernel(qmin_ref, qmax_ref, kmin_ref, kmax_ref,
                 q_ref, k_ref, vt_ref, posq_ref, posk_ref,
                 lq1_ref, lk1_ref, lq2_ref, lk2_ref, g_ref,
                 o_ref,
                 qz_ref, m_ref, l_ref, acc_ref, *, lambda_init):
    qi = pl.program_id(0)
    n_kv = posk_ref.shape[0]
    tq = q_ref.shape[0]

    q = q_ref[...]
    lane = lax.broadcasted_iota(jnp.int32, (tq, V_DIM), 1)
    zero = jnp.zeros((tq, V_DIM), q.dtype)
    for h in range(N_HEADS):
        qh = q[:, h * V_DIM:(h + 1) * V_DIM]
        qz_ref[h, 0:tq] = jnp.where(lane < HEAD_DIM, qh, zero)
        qz_ref[h, tq:2 * tq] = jnp.where(lane < HEAD_DIM, zero, qh)
    m_ref[...] = jnp.full(m_ref.shape, MASK_VALUE, _F32)
    l_ref[...] = jnp.zeros(l_ref.shape, _F32)
    acc_ref[...] = jnp.zeros(acc_ref.shape, _F32)

    q_lo = qmin_ref[qi]
    q_hi = qmax_ref[qi]

    def step(kj, masked):
        rows = pl.ds(pl.multiple_of(kj * KV_TILE, KV_TILE), KV_TILE)
        scores = [_nt_dot(k_ref[rows, h * V_DIM:(h + 1) * V_DIM], qz_ref[h])
                  for h in range(N_HEADS)]
        if masked:
            pk = jnp.broadcast_to(posk_ref[kj], (LANES, KV_TILE))
            pk = jnp.tile(pk.T, (1, 2 * tq // LANES))
            posq = posq_ref[...]
            visible = jnp.concatenate([posq, posq], axis=1) >= pk
        for h in range(N_HEADS):
            s = scores[h]
            if masked:
                s = jnp.where(visible, s, MASK_VALUE)
            m_prev = m_ref[h]
            m_new = jnp.maximum(m_prev, jnp.max(s, axis=0, keepdims=True))
            alpha = jnp.exp(m_prev - m_new)
            p = jnp.exp(s - m_new)
            l_ref[h] = alpha * l_ref[h] + jnp.sum(p, axis=0, keepdims=True)
            acc_ref[h] = alpha * acc_ref[h] + jnp.dot(
                vt_ref[h, kj], p.astype(_BF16), preferred_element_type=_F32)
            m_ref[h] = m_new

    def body(kj, carry):
        k_lo = kmin_ref[kj]
        k_hi = kmax_ref[kj]
        needed = k_lo <= q_hi
        full = k_hi <= q_lo

        @pl.when(jnp.logical_and(needed, full))
        def _():
            step(kj, False)

        @pl.when(jnp.logical_and(needed, jnp.logical_not(full)))
        def _():
            step(kj, True)

        return carry

    lax.fori_loop(0, n_kv, body, 0)

    lam = (jnp.exp(jnp.sum(lq1_ref[...] * lk1_ref[...], keepdims=True))
           - jnp.exp(jnp.sum(lq2_ref[...] * lk2_ref[...], keepdims=True))
           + lambda_init)
    for h in range(N_HEADS):
        o = acc_ref[h] / l_ref[h]
        o = o[:, 0:tq] - lam * o[:, tq:2 * tq]
        o = o * lax.rsqrt(jnp.mean(o * o, axis=0, keepdims=True) + EPS)
        o = (o * g_ref[...]) * (1.0 - lambda_init)
        o_ref[:, h * V_DIM:(h + 1) * V_DIM] = o.T.astype(o_ref.dtype)


def _attention(q, k, vt, pos_row, lq1, lk1, lq2, lk2, g_col, lambda_init):
    s = q.shape[0]
    n_q = s // Q_TILE
    n_kv = s // KV_TILE
    pos = pos_row[0]
    qmin = pos.reshape(n_q, Q_TILE).min(axis=1)
    qmax = pos.reshape(n_q, Q_TILE).max(axis=1)
    kmin = pos.reshape(n_kv, KV_TILE).min(axis=1)
    kmax = pos.reshape(n_kv, KV_TILE).max(axis=1)
    posk = pos_row.reshape(n_kv, 1, KV_TILE)
    small = lambda i, *_: (0, 0)
    resident = dict(pipeline_mode=pl.Buffered(1))
    grid_spec = pltpu.PrefetchScalarGridSpec(
        num_scalar_prefetch=4,
        grid=(n_q,),
        in_specs=[
            pl.BlockSpec((Q_TILE, ATTN_WIDTH), lambda i, *_: (i, 0)),
            pl.BlockSpec((s, ATTN_WIDTH), lambda i, *_: (0, 0), **resident),
            pl.BlockSpec((N_HEADS, n_kv, V_DIM, KV_TILE), lambda i, *_: (0, 0, 0, 0), **resident),
            pl.BlockSpec((1, Q_TILE), lambda i, *_: (0, i)),
            pl.BlockSpec((n_kv, 1, KV_TILE), lambda i, *_: (0, 0, 0)),
            pl.BlockSpec((1, HEAD_DIM), small),
            pl.BlockSpec((1, HEAD_DIM), small),
            pl.BlockSpec((1, HEAD_DIM), small),
            pl.BlockSpec((1, HEAD_DIM), small),
            pl.BlockSpec((V_DIM, 1), small),
        ],
        out_specs=pl.BlockSpec((Q_TILE, ATTN_WIDTH), lambda i, *_: (i, 0)),
        scratch_shapes=[
            pltpu.VMEM((N_HEADS, 2 * Q_TILE, V_DIM), _BF16),
            pltpu.VMEM((N_HEADS, 1, 2 * Q_TILE), _F32),
            pltpu.VMEM((N_HEADS, 1, 2 * Q_TILE), _F32),
            pltpu.VMEM((N_HEADS, V_DIM, 2 * Q_TILE), _F32),
        ],
    )
    return pl.pallas_call(
        functools.partial(_attn_kernel, lambda_init=lambda_init),
        grid_spec=grid_spec,
        out_shape=jax.ShapeDtypeStruct((s, ATTN_WIDTH), _BF16),
        compiler_params=pltpu.CompilerParams(
            dimension_semantics=("arbitrary",), vmem_limit_bytes=VMEM_LIMIT_BYTES),
        name="diff_attn",
    )(qmin, qmax, kmin, kmax, q, k, vt, pos_row, posk, lq1, lk1, lq2, lk2, g_col)


def _mix_kernel(c_ref, chalo_ref, att_ref, x_ref, dww_ref, dwb_ref, lng_ref, lnb_ref,
                wo_ref, g_ref, o_ref, ext_ref, cact_ref):
    i = pl.program_id(0)
    tm = c_ref.shape[0]
    ext_ref[0:CONV_HALO] = jnp.where(i > 0, chalo_ref[...], 0.0)
    ext_ref[CONV_HALO:CONV_HALO + tm] = c_ref[...]

    bias = dwb_ref[...]
    ln_g = lng_ref[...]
    ln_b = lnb_ref[...]
    first_tap = CONV_HALO - (CONV_K - 1)

    for r0 in range(0, tm, CONV_ROWS):
        acc = jnp.broadcast_to(bias, (CONV_ROWS, CONV_WIDTH))
        for j in range(CONV_K):
            start = r0 + first_tap + j
            acc = acc + dww_ref[j:j + 1, :] * ext_ref[start:start + CONV_ROWS, :]
        mu = jnp.mean(acc, axis=-1, keepdims=True)
        d = acc - mu
        var = jnp.mean(d * d, axis=-1, keepdims=True)
        y = d * lax.rsqrt(var + EPS) * ln_g + ln_b
        cact_ref[r0:r0 + CONV_ROWS, :] = (y * jax.nn.sigmoid(y)).astype(cact_ref.dtype)

    mix = (jnp.dot(att_ref[...], wo_ref[0:ATTN_WIDTH, :], preferred_element_type=_F32)
           + jnp.dot(cact_ref[...], wo_ref[ATTN_WIDTH:, :], preferred_element_type=_F32))
    o_ref[...] = x_ref[...] + mix * _rms_scale(mix) * g_ref[...]


def _mix(c, att, x, dww, dwb, lng, lnb, wo, g):
    s = x.shape[0]
    n_tiles = s // ROW_TILE
    halo_blocks = ROW_TILE // CONV_HALO
    const = lambda i: (0, 0)
    row = lambda i: (i, 0)
    return pl.pallas_call(
        _mix_kernel,
        grid=(n_tiles,),
        in_specs=[
            pl.BlockSpec((ROW_TILE, CONV_WIDTH), row),
            pl.BlockSpec((CONV_HALO, CONV_WIDTH), lambda i: (jnp.maximum(i * halo_blocks - 1, 0), 0)),
            pl.BlockSpec((ROW_TILE, ATTN_WIDTH), row),
            pl.BlockSpec((ROW_TILE, D_MODEL), row),
            pl.BlockSpec((CONV_K, CONV_WIDTH), const),
            pl.BlockSpec((1, CONV_WIDTH), const),
            pl.BlockSpec((1, CONV_WIDTH), const),
            pl.BlockSpec((1, CONV_WIDTH), const),
            pl.BlockSpec((D_MODEL, D_MODEL), const),
            pl.BlockSpec((1, D_MODEL), const),
        ],
        out_specs=pl.BlockSpec((ROW_TILE, D_MODEL), row),
        out_shape=jax.ShapeDtypeStruct((s, D_MODEL), _F32),
        scratch_shapes=[
            pltpu.VMEM((CONV_HALO + ROW_TILE, CONV_WIDTH), _F32),
            pltpu.VMEM((ROW_TILE, CONV_WIDTH), _BF16),
        ],
        compiler_params=pltpu.CompilerParams(
            dimension_semantics=("arbitrary",), vmem_limit_bytes=VMEM_LIMIT_BYTES),
        name="conv_mix",
    )(c, c, att, x, dww, dwb, lng, lnb, wo, g)


def _ffn_kernel(x_ref, xhalo_ref, pre_g_ref, wup_ref, dww_ref, dwb_ref, wdn_ref, post_g_ref,
                o_ref, ug_ref, uu_ref, act_ref):
    i = pl.program_id(0)
    tm = x_ref.shape[0]
    x = x_ref[...]
    xe = jnp.concatenate([xhalo_ref[...], x], axis=0)
    h = (xe * _rms_scale(xe) * pre_g_ref[...]).astype(_BF16)

    def conv(u_ref, col):
        w = dww_ref[:, col:col + FF_CHUNK]
        acc = dwb_ref[:, col:col + FF_CHUNK] + w[2:3] * u_ref[FFN_HALO:FFN_HALO + tm, :]
        acc = acc + w[1:2] * u_ref[FFN_HALO - 1:FFN_HALO - 1 + tm, :]
        return acc + w[0:1] * u_ref[FFN_HALO - 2:FFN_HALO - 2 + tm, :]

    def store_u(u_ref, u):
        u_ref[0:FFN_HALO] = jnp.where(i > 0, u[0:FFN_HALO], 0.0)
        u_ref[FFN_HALO:] = u[FFN_HALO:]

    for c in range(D_FF // FF_CHUNK):
        gcol = c * FF_CHUNK
        ucol = D_FF + c * FF_CHUNK
        store_u(ug_ref, jnp.dot(h, wup_ref[:, gcol:gcol + FF_CHUNK], preferred_element_type=_F32))
        store_u(uu_ref, jnp.dot(h, wup_ref[:, ucol:ucol + FF_CHUNK], preferred_element_type=_F32))
        gate = conv(ug_ref, gcol)
        up = conv(uu_ref, ucol)
        act_ref[:, gcol:gcol + FF_CHUNK] = (jax.nn.gelu(gate, approximate=True) * up).astype(_BF16)

    y = jnp.dot(act_ref[...], wdn_ref[...], preferred_element_type=_F32)
    o_ref[...] = x + y * _rms_scale(y) * post_g_ref[...]


def _ffn(x, pre_g, wup, dww, dwb, wdn, post_g):
    s = x.shape[0]
    n_tiles = s // ROW_TILE
    halo_blocks = ROW_TILE // FFN_HALO
    const = lambda i: (0, 0)
    row = lambda i: (i, 0)
    resident = dict(pipeline_mode=pl.Buffered(1))
    return pl.pallas_call(
        _ffn_kernel,
        grid=(n_tiles,),
        in_specs=[
            pl.BlockSpec((ROW_TILE, D_MODEL), row),
            pl.BlockSpec((FFN_HALO, D_MODEL), lambda i: (jnp.maximum(i * halo_blocks - 1, 0), 0)),
            pl.BlockSpec((1, D_MODEL), const),
            pl.BlockSpec((D_MODEL, 2 * D_FF), const, **resident),
            pl.BlockSpec((FFN_CONV_K, 2 * D_FF), const),
            pl.BlockSpec((1, 2 * D_FF), const),
            pl.BlockSpec((D_FF, D_MODEL), const, **resident),
            pl.BlockSpec((1, D_MODEL), const),
        ],
        out_specs=pl.BlockSpec((ROW_TILE, D_MODEL), row),
        out_shape=jax.ShapeDtypeStruct((s, D_MODEL), _F32),
        scratch_shapes=[
            pltpu.VMEM((FFN_HALO + ROW_TILE, FF_CHUNK), _F32),
            pltpu.VMEM((FFN_HALO + ROW_TILE, FF_CHUNK), _F32),
            pltpu.VMEM((ROW_TILE, D_FF), _BF16),
        ],
        compiler_params=pltpu.CompilerParams(
            dimension_semantics=("arbitrary",), vmem_limit_bytes=VMEM_LIMIT_BYTES),
        name="conv_ffn",
    )(x, x, pre_g, wup, dww, dwb, wdn, post_g)


def kernel(x, positions, attn_pre_g, attn_post_g, w_in, lambda_q1, lambda_k1, lambda_q2, lambda_k2,
           subln_g, conv_dw_w, conv_dw_b, conv_ln_g, conv_ln_b, w_out, ffn_pre_g, ffn_post_g, w_up,
           ffn_dw_w, ffn_dw_b, w_down):
    bsz, s, d = x.shape
    depth = w_in.shape[0]
    assert bsz == 1 and d == D_MODEL
    assert s % ROW_TILE == 0 and s % KV_TILE == 0 and s % Q_TILE == 0 and ROW_TILE == KV_TILE

    pos_row = positions.reshape(1, s).astype(jnp.int32)
    pos_col = pos_row.reshape(s, 1)
    half = HEAD_DIM // 2
    inv_freq = ROPE_THETA ** (-jnp.arange(0, HEAD_DIM, 2, dtype=_F32) / HEAD_DIM)
    invf = jnp.tile(inv_freq, LANES // half).reshape(1, LANES)

    row = lambda v: v.reshape(1, -1).astype(_F32)
    xs = x[0]
    for l in range(depth):
        lambda_init = 0.8 - 0.6 * math.exp(-0.3 * l)
        w = w_in[l].astype(_BF16)
        wq = w[:, :ATTN_WIDTH]
        wk = w[:, ATTN_WIDTH:2 * ATTN_WIDTH]
        wvt = w[:, 2 * ATTN_WIDTH:3 * ATTN_WIDTH].T
        wcg = w[:, 3 * ATTN_WIDTH:]
        q, k, vt, c = _in_proj(xs, pos_col, row(attn_pre_g[l]), invf, wq, wk, wvt, wcg)
        att = _attention(q, k, vt, pos_row, row(lambda_q1[l]), row(lambda_k1[l]),
                         row(lambda_q2[l]), row(lambda_k2[l]),
                         subln_g[l].reshape(V_DIM, 1).astype(_F32), lambda_init)
        xs = _mix(c, att, xs, conv_dw_w[l].astype(_F32), row(conv_dw_b[l]), row(conv_ln_g[l]),
                  row(conv_ln_b[l]), w_out[l].astype(_BF16), row(attn_post_g[l]))
        xs = _ffn(xs, row(ffn_pre_g[l]), w_up[l].astype(_BF16), ffn_dw_w[l].astype(_F32),
                  row(ffn_dw_b[l]), w_down[l].astype(_BF16), row(ffn_post_g[l]))
    return xs[None]
```

```python
import functools
import math

import jax
import jax.numpy as jnp
from jax import lax
from jax.experimental import pallas as pl
from jax.experimental.pallas import tpu as pltpu

D_MODEL = 1024
ATTN_WIDTH = 512
CONV_WIDTH = 512
HEAD_DIM = 64
V_DIM = 2 * HEAD_DIM
N_HEADS = ATTN_WIDTH // V_DIM
CONV_K = 31
FFN_CONV_K = 3
D_FF = 2816
ROPE_THETA = 10000.0
EPS = 1e-6

LANES = 128
SUBLANES = 8
MXU_COLS = 256
VMEM_LIMIT_BYTES = 56 * 1024 * 1024

ROW_TILE = 512
Q_TILE = 256
KV_TILE = 512
CONV_HALO = 32
CONV_ROWS = 32
FFN_HALO = 8
FF_CHUNK = 256

MASK_VALUE = -1e30

_BF16 = jnp.bfloat16
_F32 = jnp.float32


def _rms_scale(x):
    return lax.rsqrt(jnp.mean(x * x, axis=-1, keepdims=True) + EPS)


def _nt_dot(a, b):
    return lax.dot_general(a, b, (((1,), (1,)), ((), ())), preferred_element_type=_F32)


def _in_proj_kernel(x_ref, pos_ref, g_ref, invf_ref, wq_ref, wk_ref, wvt_ref, wcg_ref,
                    q_ref, k_ref, vt_ref, c_ref):
    x = x_ref[...]
    h = (x * _rms_scale(x) * g_ref[...]).astype(_BF16)

    ang = pos_ref[...].astype(_F32) * invf_ref[...]
    cos = jnp.cos(ang)
    sin = jnp.sin(ang)
    lane = lax.broadcasted_iota(jnp.int32, ang.shape, 1)
    first_half = (lane & (HEAD_DIM // 2)) == 0
    sin_first = jnp.where(first_half, -sin, 0.0)
    sin_second = jnp.where(first_half, 0.0, sin)

    def rope(t):
        from_right = pltpu.roll(t, LANES - HEAD_DIM // 2, 1)
        from_left = pltpu.roll(t, HEAD_DIM // 2, 1)
        return t * cos + from_right * sin_first + from_left * sin_second

    scale = HEAD_DIM ** -0.5 * math.log2(math.e)
    q = jnp.dot(h, wq_ref[...], preferred_element_type=_F32)
    k = jnp.dot(h, wk_ref[...], preferred_element_type=_F32)
    for j in range(N_HEADS):
        sl = slice(j * LANES, (j + 1) * LANES)
        q_ref[:, sl] = (rope(q[:, sl]) * scale).astype(_BF16)
        k_ref[:, sl] = rope(k[:, sl]).astype(_BF16)

    vt = _nt_dot(wvt_ref[...], h)
    for j in range(N_HEADS):
        vt_ref[j, 0] = vt[j * V_DIM:(j + 1) * V_DIM, :].astype(_BF16)

    cg = jnp.dot(h, wcg_ref[...], preferred_element_type=_F32)
    c_ref[...] = cg[:, :CONV_WIDTH] * jax.nn.sigmoid(cg[:, CONV_WIDTH:])


def _in_proj(x, pos_col, g, invf, wq, wk, wvt, wcg):
    s = x.shape[0]
    n_tiles = s // ROW_TILE
    const = lambda i: (0, 0)
    return pl.pallas_call(
        _in_proj_kernel,
        grid=(n_tiles,),
        in_specs=[
            pl.BlockSpec((ROW_TILE, D_MODEL), lambda i: (i, 0)),
            pl.BlockSpec((ROW_TILE, 1), lambda i: (i, 0)),
            pl.BlockSpec((1, D_MODEL), const),
            pl.BlockSpec((1, LANES), const),
            pl.BlockSpec((D_MODEL, ATTN_WIDTH), const),
            pl.BlockSpec((D_MODEL, ATTN_WIDTH), const),
            pl.BlockSpec((ATTN_WIDTH, D_MODEL), const),
            pl.BlockSpec((D_MODEL, 2 * CONV_WIDTH), const),
        ],
        out_specs=[
            pl.BlockSpec((ROW_TILE, ATTN_WIDTH), lambda i: (i, 0)),
            pl.BlockSpec((ROW_TILE, ATTN_WIDTH), lambda i: (i, 0)),
            pl.BlockSpec((N_HEADS, 1, V_DIM, ROW_TILE), lambda i: (0, i, 0, 0)),
            pl.BlockSpec((ROW_TILE, CONV_WIDTH), lambda i: (i, 0)),
        ],
        out_shape=[
            jax.ShapeDtypeStruct((s, ATTN_WIDTH), _BF16),
            jax.ShapeDtypeStruct((s, ATTN_WIDTH), _BF16),
            jax.ShapeDtypeStruct((N_HEADS, n_tiles, V_DIM, ROW_TILE), _BF16),
            jax.ShapeDtypeStruct((s, CONV_WIDTH), _F32),
        ],
        compiler_params=pltpu.CompilerParams(
            dimension_semantics=("arbitrary",), vmem_limit_bytes=VMEM_LIMIT_BYTES),
        name="in_proj",
    )(x, pos_col, g, invf, wq, wk, wvt, wcg)


def _attn_kernel(ids_ref, nfull_ref, nneed_ref,
                 q_ref, k_ref, vt_ref, posq_ref, posk_ref,
                 lq1_ref, lk1_ref, lq2_ref, lk2_ref, g_ref,
                 o_ref,
                 qz_ref, s_ref, mblk_ref, m_ref, l_ref, acc_ref, *, lambda_init):
    qi = pl.program_id(0)
    n_kv = posk_ref.shape[0]
    tq = q_ref.shape[0]

    q = q_ref[...]
    lane = lax.broadcasted_iota(jnp.int32, (tq, V_DIM), 1)
    zero = jnp.zeros((tq, V_DIM), q.dtype)
    for h in range(N_HEADS):
        qh = q[:, h * V_DIM:(h + 1) * V_DIM]
        qz_ref[h, 0:tq] = jnp.where(lane < HEAD_DIM, qh, zero)
        qz_ref[h, tq:2 * tq] = jnp.where(lane < HEAD_DIM, zero, qh)
    m_ref[...] = jnp.full(m_ref.shape, MASK_VALUE, _F32)
    l_ref[...] = jnp.zeros(l_ref.shape, _F32)
    acc_ref[...] = jnp.zeros(acc_ref.shape, _F32)

    n_full = nfull_ref[qi]
    n_need = nneed_ref[qi]
    base = qi * n_kv

    def visibility(kj):
        pk = jnp.broadcast_to(posk_ref[kj], (LANES, KV_TILE))
        pk = jnp.tile(pk.T, (1, 2 * tq // LANES))
        posq = posq_ref[...]
        return jnp.concatenate([posq, posq], axis=1) >= pk

    def score_head(h, kj, slot, visible):
        rows = pl.ds(pl.multiple_of(kj * KV_TILE, KV_TILE), KV_TILE)
        s = _nt_dot(k_ref[rows, h * V_DIM:(h + 1) * V_DIM], qz_ref[h])
        if visible is not None:
            s = jnp.where(visible, s, MASK_VALUE)
        mblk_ref[slot, h] = jnp.max(s, axis=0, keepdims=True)
        s_ref[slot, h] = s

    def value_head(h, kj, slot):
        m_prev = m_ref[h]
        m_new = jnp.maximum(m_prev, mblk_ref[slot, h])
        alpha = jnp.exp2(m_prev - m_new)
        p = jnp.exp2(s_ref[slot, h] - m_new)
        l_ref[h] = alpha * l_ref[h] + jnp.sum(p, axis=0, keepdims=True)
        acc_ref[h] = alpha * acc_ref[h] + jnp.dot(
            vt_ref[h, kj], p.astype(_BF16), preferred_element_type=_F32)
        m_ref[h] = m_new

    def stage(score_j, score_slot, masked, value_j, value_slot):
        if score_j is not None:
            score_kj = ids_ref[base + score_j]
            visible = visibility(score_kj) if masked else None
        if value_j is not None:
            value_kj = ids_ref[base + value_j]
        for h in range(N_HEADS):
            if score_j is not None:
                score_head(h, score_kj, score_slot, visible)
            if value_j is not None:
                value_head(h, value_kj, value_slot)

    @pl.when(n_full > 0)
    def _():
        stage(0, 0, False, None, None)

    @pl.when(n_full == 0)
    def _():
        stage(0, 0, True, None, None)

    def body(j, carry):
        next_masked = j + 1 >= n_full
        for slot in range(2):
            for masked in (False, True):
                @pl.when(jnp.logical_and((j & 1) == slot, next_masked == masked))
                def _():
                    stage(j + 1, 1 - slot, masked, j, slot)
        return carry

    lax.fori_loop(0, n_need - 1, body, 0)

    last = n_need - 1
    for slot in range(2):
        @pl.when((last & 1) == slot)
        def _():
            stage(None, None, False, last, slot)

    lam = (jnp.exp(jnp.sum(lq1_ref[...] * lk1_ref[...], keepdims=True))
           - jnp.exp(jnp.sum(lq2_ref[...] * lk2_ref[...], keepdims=True))
           + lambda_init)
    for h in range(N_HEADS):
        o = acc_ref[h] / l_ref[h]
        o = o[:, 0:tq] - lam * o[:, tq:2 * tq]
        o = o * lax.rsqrt(jnp.mean(o * o, axis=0, keepdims=True) + EPS)
        o = (o * g_ref[...]) * (1.0 - lambda_init)
        o_ref[:, h * V_DIM:(h + 1) * V_DIM] = o.T.astype(o_ref.dtype)


def _attention(q, k, vt, pos_row, lq1, lk1, lq2, lk2, g_col, lambda_init):
    s = q.shape[0]
    n_q = s // Q_TILE
    n_kv = s // KV_TILE
    pos = pos_row[0]
    qmin = pos.reshape(n_q, Q_TILE).min(axis=1)
    qmax = pos.reshape(n_q, Q_TILE).max(axis=1)
    kmin = pos.reshape(n_kv, KV_TILE).min(axis=1)
    kmax = pos.reshape(n_kv, KV_TILE).max(axis=1)
    needed = kmin[None, :] <= qmax[:, None]
    full = jnp.logical_and(needed, kmax[None, :] <= qmin[:, None])
    partial = jnp.logical_and(needed, jnp.logical_not(full))
    n_full = full.sum(axis=1).astype(jnp.int32)
    n_need = needed.sum(axis=1).astype(jnp.int32)
    rank = jnp.where(full, jnp.cumsum(full, axis=1) - 1,
                     jnp.where(partial, n_full[:, None] + jnp.cumsum(partial, axis=1) - 1, n_kv))
    tile_id = jnp.arange(n_kv, dtype=jnp.int32)
    ids = jnp.sum(jnp.where(rank[:, :, None] == tile_id[None, None, :], tile_id[None, :, None], 0),
                  axis=1).astype(jnp.int32).reshape(-1)
    posk = pos_row.reshape(n_kv, 1, KV_TILE)
    small = lambda i, *_: (0, 0)
    resident = dict(pipeline_mode=pl.Buffered(1))
    grid_spec = pltpu.PrefetchScalarGridSpec(
        num_scalar_prefetch=3,
        grid=(n_q,),
        in_specs=[
            pl.BlockSpec((Q_TILE, ATTN_WIDTH), lambda i, *_: (i, 0)),
            pl.BlockSpec((s, ATTN_WIDTH), lambda i, *_: (0, 0), **resident),
            pl.BlockSpec((N_HEADS, n_kv, V_DIM, KV_TILE), lambda i, *_: (0, 0, 0, 0), **resident),
            pl.BlockSpec((1, Q_TILE), lambda i, *_: (0, i)),
            pl.BlockSpec((n_kv, 1, KV_TILE), lambda i, *_: (0, 0, 0)),
            pl.BlockSpec((1, HEAD_DIM), small),
            pl.BlockSpec((1, HEAD_DIM), small),
            pl.BlockSpec((1, HEAD_DIM), small),
            pl.BlockSpec((1, HEAD_DIM), small),
            pl.BlockSpec((V_DIM, 1), small),
        ],
        out_specs=pl.BlockSpec((Q_TILE, ATTN_WIDTH), lambda i, *_: (i, 0)),
        scratch_shapes=[
            pltpu.VMEM((N_HEADS, 2 * Q_TILE, V_DIM), _BF16),
            pltpu.VMEM((2, N_HEADS, KV_TILE, 2 * Q_TILE), _F32),
            pltpu.VMEM((2, N_HEADS, 1, 2 * Q_TILE), _F32),
            pltpu.VMEM((N_HEADS, 1, 2 * Q_TILE), _F32),
            pltpu.VMEM((N_HEADS, 1, 2 * Q_TILE), _F32),
            pltpu.VMEM((N_HEADS, V_DIM, 2 * Q_TILE), _F32),
        ],
    )
    return pl.pallas_call(
        functools.partial(_attn_kernel, lambda_init=lambda_init),
        grid_spec=grid_spec,
        out_shape=jax.ShapeDtypeStruct((s, ATTN_WIDTH), _BF16),
        compiler_params=pltpu.CompilerParams(
            dimension_semantics=("arbitrary",), vmem_limit_bytes=VMEM_LIMIT_BYTES),
        name="diff_attn",
    )(ids, n_full, n_need, q, k, vt, pos_row, posk, lq1, lk1, lq2, lk2, g_col)


def _mix_kernel(c_ref, chalo_ref, att_ref, x_ref, dww_ref, dwb_ref, lng_ref, lnb_ref,
                wo_ref, g_ref, o_ref, ext_ref, cact_ref):
    i = pl.program_id(0)
    tm = c_ref.shape[0]
    ext_ref[0:CONV_HALO] = jnp.where(i > 0, chalo_ref[...], 0.0)
    ext_ref[CONV_HALO:CONV_HALO + tm] = c_ref[...]

    bias = dwb_ref[...]
    ln_g = lng_ref[...]
    ln_b = lnb_ref[...]
    first_tap = CONV_HALO - (CONV_K - 1)

    for r0 in range(0, tm, CONV_ROWS):
        acc = jnp.broadcast_to(bias, (CONV_ROWS, CONV_WIDTH))
        for shift in range(SUBLANES):
            rows = CONV_ROWS if shift == 0 else CONV_ROWS + SUBLANES
            part = None
            for j in range(CONV_K):
                offset = first_tap + j
                if offset % SUBLANES != shift:
                    continue
                start = r0 + offset - shift
                term = dww_ref[j:j + 1, :] * ext_ref[start:start + rows, :]
                part = term if part is None else part + term
            if shift:
                part = pltpu.roll(part, rows - shift, 0)[0:CONV_ROWS]
            acc = acc + part
        mu = jnp.mean(acc, axis=-1, keepdims=True)
        d = acc - mu
        var = jnp.mean(d * d, axis=-1, keepdims=True)
        y = d * lax.rsqrt(var + EPS) * ln_g + ln_b
        cact_ref[r0:r0 + CONV_ROWS, :] = (y * jax.nn.sigmoid(y)).astype(cact_ref.dtype)

    mix = (jnp.dot(att_ref[...], wo_ref[0:ATTN_WIDTH, :], preferred_element_type=_F32)
           + jnp.dot(cact_ref[...], wo_ref[ATTN_WIDTH:, :], preferred_element_type=_F32))
    o_ref[...] = x_ref[...] + mix * _rms_scale(mix) * g_ref[...]


def _mix(c, att, x, dww, dwb, lng, lnb, wo, g):
    s = x.shape[0]
    n_tiles = s // ROW_TILE
    halo_blocks = ROW_TILE // CONV_HALO
    const = lambda i: (0, 0)
    row = lambda i: (i, 0)
    return pl.pallas_call(
        _mix_kernel,
        grid=(n_tiles,),
        in_specs=[
            pl.BlockSpec((ROW_TILE, CONV_WIDTH), row),
            pl.BlockSpec((CONV_HALO, CONV_WIDTH), lambda i: (jnp.maximum(i * halo_blocks - 1, 0), 0)),
            pl.BlockSpec((ROW_TILE, ATTN_WIDTH), row),
            pl.BlockSpec((ROW_TILE, D_MODEL), row),
            pl.BlockSpec((CONV_K, CONV_WIDTH), const),
            pl.BlockSpec((1, CONV_WIDTH), const),
            pl.BlockSpec((1, CONV_WIDTH), const),
            pl.BlockSpec((1, CONV_WIDTH), const),
            pl.BlockSpec((D_MODEL, D_MODEL), const),
            pl.BlockSpec((1, D_MODEL), const),
        ],
        out_specs=pl.BlockSpec((ROW_TILE, D_MODEL), row),
        out_shape=jax.ShapeDtypeStruct((s, D_MODEL), _F32),
        scratch_shapes=[
            pltpu.VMEM((CONV_HALO + ROW_TILE, CONV_WIDTH), _F32),
            pltpu.VMEM((ROW_TILE, CONV_WIDTH), _BF16),
        ],
        compiler_params=pltpu.CompilerParams(
            dimension_semantics=("arbitrary",), vmem_limit_bytes=VMEM_LIMIT_BYTES),
        name="conv_mix",
    )(c, c, att, x, dww, dwb, lng, lnb, wo, g)


def _ffn_kernel(x_ref, xhalo_ref, pre_g_ref, wup_ref, dww_ref, dwb_ref, wdn_ref, post_g_ref,
                o_ref, ug_ref, uu_ref, act_ref):
    i = pl.program_id(0)
    tm = x_ref.shape[0]
    x = x_ref[...]
    xe = jnp.concatenate([xhalo_ref[...], x], axis=0)
    h = (xe * _rms_scale(xe) * pre_g_ref[...]).astype(_BF16)

    def conv(u_ref, col):
        w = dww_ref[:, col:col + FF_CHUNK]
        acc = dwb_ref[:, col:col + FF_CHUNK] + w[2:3] * u_ref[FFN_HALO:FFN_HALO + tm, :]
        acc = acc + w[1:2] * u_ref[FFN_HALO - 1:FFN_HALO - 1 + tm, :]
        return acc + w[0:1] * u_ref[FFN_HALO - 2:FFN_HALO - 2 + tm, :]

    def store_u(u_ref, u):
        u_ref[0:FFN_HALO] = jnp.where(i > 0, u[0:FFN_HALO], 0.0)
        u_ref[FFN_HALO:] = u[FFN_HALO:]

    for c in range(D_FF // FF_CHUNK):
        gcol = c * FF_CHUNK
        ucol = D_FF + c * FF_CHUNK
        store_u(ug_ref, jnp.dot(h, wup_ref[:, gcol:gcol + FF_CHUNK], preferred_element_type=_F32))
        store_u(uu_ref, jnp.dot(h, wup_ref[:, ucol:ucol + FF_CHUNK], preferred_element_type=_F32))
        gate = conv(ug_ref, gcol)
        up = conv(uu_ref, ucol)
        act_ref[:, gcol:gcol + FF_CHUNK] = (jax.nn.gelu(gate, approximate=True) * up).astype(_BF16)

    y = jnp.dot(act_ref[...], wdn_ref[...], preferred_element_type=_F32)
    o_ref[...] = x + y * _rms_scale(y) * post_g_ref[...]


def _ffn(x, pre_g, wup, dww, dwb, wdn, post_g):
    s = x.shape[0]
    n_tiles = s // ROW_TILE
    halo_blocks = ROW_TILE // FFN_HALO
    const = lambda i: (0, 0)
    row = lambda i: (i, 0)
    resident = dict(pipeline_mode=pl.Buffered(1))
    return pl.pallas_call(
        _ffn_kernel,
        grid=(n_tiles,),
        in_specs=[
            pl.BlockSpec((ROW_TILE, D_MODEL), row),
            pl.BlockSpec((FFN_HALO, D_MODEL), lambda i: (jnp.maximum(i * halo_blocks - 1, 0), 0)),
            pl.BlockSpec((1, D_MODEL), const),
            pl.BlockSpec((D_MODEL, 2 * D_FF), const, **resident),
            pl.BlockSpec((FFN_CONV_K, 2 * D_FF), const),
            pl.BlockSpec((1, 2 * D_FF), const),
            pl.BlockSpec((D_FF, D_MODEL), const, **resident),
            pl.BlockSpec((1, D_MODEL), const),
        ],
        out_specs=pl.BlockSpec((ROW_TILE, D_MODEL), row),
        out_shape=jax.ShapeDtypeStruct((s, D_MODEL), _F32),
        scratch_shapes=[
            pltpu.VMEM((FFN_HALO + ROW_TILE, FF_CHUNK), _F32),
            pltpu.VMEM((FFN_HALO + ROW_TILE, FF_CHUNK), _F32),
            pltpu.VMEM((ROW_TILE, D_FF), _BF16),
        ],
        compiler_params=pltpu.CompilerParams(
            dimension_semantics=("arbitrary",), vmem_limit_bytes=VMEM_LIMIT_BYTES),
        name="conv_ffn",
    )(x, x, pre_g, wup, dww, dwb, wdn, post_g)


def kernel(x, positions, attn_pre_g, attn_post_g, w_in, lambda_q1, lambda_k1, lambda_q2, lambda_k2,
           subln_g, conv_dw_w, conv_dw_b, conv_ln_g, conv_ln_b, w_out, ffn_pre_g, ffn_post_g, w_up,
           ffn_dw_w, ffn_dw_b, w_down):
    bsz, s, d = x.shape
    depth = w_in.shape[0]
    assert bsz == 1 and d == D_MODEL
    assert s % ROW_TILE == 0 and s % KV_TILE == 0 and s % Q_TILE == 0 and ROW_TILE == KV_TILE

    pos_row = positions.reshape(1, s).astype(jnp.int32)
    pos_col = pos_row.reshape(s, 1)
    half = HEAD_DIM // 2
    inv_freq = ROPE_THETA ** (-jnp.arange(0, HEAD_DIM, 2, dtype=_F32) / HEAD_DIM)
    invf = jnp.tile(inv_freq, LANES // half).reshape(1, LANES)

    row = lambda v: v.reshape(1, -1).astype(_F32)
    xs = x[0]
    for l in range(depth):
        lambda_init = 0.8 - 0.6 * math.exp(-0.3 * l)
        w = w_in[l].astype(_BF16)
        wq = w[:, :ATTN_WIDTH]
        wk = w[:, ATTN_WIDTH:2 * ATTN_WIDTH]
        wvt = w[:, 2 * ATTN_WIDTH:3 * ATTN_WIDTH].T
        wcg = w[:, 3 * ATTN_WIDTH:]
        q, k, vt, c = _in_proj(xs, pos_col, row(attn_pre_g[l]), invf, wq, wk, wvt, wcg)
        att = _attention(q, k, vt, pos_row, row(lambda_q1[l]), row(lambda_k1[l]),
                         row(lambda_q2[l]), row(lambda_k2[l]),
                         subln_g[l].reshape(V_DIM, 1).astype(_F32), lambda_init)
        xs = _mix(c, att, xs, conv_dw_w[l].astype(_F32), row(conv_dw_b[l]), row(conv_ln_g[l]),
                  row(conv_ln_b[l]), w_out[l].astype(_BF16), row(attn_post_g[l]))
        xs = _ffn(xs, row(ffn_pre_g[l]), w_up[l].astype(_BF16), ffn_dw_w[l].astype(_F32),
                  row(ffn_dw_b[l]), w_down[l].astype(_BF16), row(ffn_post_g[l]))
    return xs[None]
```

```python
import functools
import math

import jax
import jax.numpy as jnp
from jax import lax
from jax.experimental import pallas as pl
from jax.experimental.pallas import tpu as pltpu

D_MODEL = 1024
ATTN_WIDTH = 512
CONV_WIDTH = 512
HEAD_DIM = 64
V_DIM = 2 * HEAD_DIM
N_HEADS = ATTN_WIDTH // V_DIM
CONV_K = 31
FFN_CONV_K = 3
D_FF = 2816
ROPE_THETA = 10000.0
EPS = 1e-6

LANES = 128
SUBLANES = 8
MXU_COLS = 256
VMEM_LIMIT_BYTES = 56 * 1024 * 1024

ROW_TILE = 512
Q_TILE = 1024
HEAD_GROUP = 2
SCORE_CHUNK = 512
KV_TILE = 512
CONV_HALO = 32
CONV_ROWS = 128
NORM_ROWS = 64
FFN_TILE = 512
FFN_HALO = 8
FF_CHUNK = 256

MASK_VALUE = -1e30

_BF16 = jnp.bfloat16
_F32 = jnp.float32


def _rms_scale(x):
    return lax.rsqrt(jnp.mean(x * x, axis=-1, keepdims=True) + EPS)


def _nt_dot(a, b):
    return lax.dot_general(a, b, (((1,), (1,)), ((), ())), preferred_element_type=_F32)


def _in_proj_kernel(x_ref, pos_ref, g_ref, invf_ref, wq_ref, wk_ref, wvt_ref, wcg_ref,
                    q_ref, k_ref, vt_ref, c_ref):
    x = x_ref[...]
    h = (x * _rms_scale(x) * g_ref[...]).astype(_BF16)

    cg = jnp.dot(h, wcg_ref[...], preferred_element_type=_F32)

    ang = pos_ref[...].astype(_F32) * invf_ref[...]
    cos = jnp.cos(ang)
    sin = jnp.sin(ang)
    lane = lax.broadcasted_iota(jnp.int32, ang.shape, 1)
    first_half = (lane & (HEAD_DIM // 2)) == 0
    sin_first = jnp.where(first_half, -sin, 0.0)
    sin_second = jnp.where(first_half, 0.0, sin)

    def rope(t):
        from_right = pltpu.roll(t, LANES - HEAD_DIM // 2, 1)
        from_left = pltpu.roll(t, HEAD_DIM // 2, 1)
        return t * cos + from_right * sin_first + from_left * sin_second

    scale = HEAD_DIM ** -0.5 * math.log2(math.e)
    q = jnp.dot(h, wq_ref[...], preferred_element_type=_F32)
    k = jnp.dot(h, wk_ref[...], preferred_element_type=_F32)
    c_ref[...] = cg[:, :CONV_WIDTH] * jax.nn.sigmoid(cg[:, CONV_WIDTH:])
    vt = _nt_dot(wvt_ref[...], h)
    for j in range(N_HEADS):
        sl = slice(j * LANES, (j + 1) * LANES)
        q_ref[:, sl] = (rope(q[:, sl]) * scale).astype(_BF16)
        k_ref[:, sl] = rope(k[:, sl]).astype(_BF16)
    for j in range(N_HEADS):
        vt_ref[j, 0] = vt[j * V_DIM:(j + 1) * V_DIM, :].astype(_BF16)


def _in_proj(x, pos_col, g, invf, wq, wk, wvt, wcg):
    s = x.shape[0]
    n_tiles = s // ROW_TILE
    const = lambda i: (0, 0)
    return pl.pallas_call(
        _in_proj_kernel,
        grid=(n_tiles,),
        in_specs=[
            pl.BlockSpec((ROW_TILE, D_MODEL), lambda i: (i, 0)),
            pl.BlockSpec((ROW_TILE, 1), lambda i: (i, 0)),
            pl.BlockSpec((1, D_MODEL), const),
            pl.BlockSpec((1, LANES), const),
            pl.BlockSpec((D_MODEL, ATTN_WIDTH), const),
            pl.BlockSpec((D_MODEL, ATTN_WIDTH), const),
            pl.BlockSpec((ATTN_WIDTH, D_MODEL), const),
            pl.BlockSpec((D_MODEL, 2 * CONV_WIDTH), const),
        ],
        out_specs=[
            pl.BlockSpec((ROW_TILE, ATTN_WIDTH), lambda i: (i, 0)),
            pl.BlockSpec((ROW_TILE, ATTN_WIDTH), lambda i: (i, 0)),
            pl.BlockSpec((N_HEADS, 1, V_DIM, ROW_TILE), lambda i: (0, i, 0, 0)),
            pl.BlockSpec((ROW_TILE, CONV_WIDTH), lambda i: (i, 0)),
        ],
        out_shape=[
            jax.ShapeDtypeStruct((s, ATTN_WIDTH), _BF16),
            jax.ShapeDtypeStruct((s, ATTN_WIDTH), _BF16),
            jax.ShapeDtypeStruct((N_HEADS, n_tiles, V_DIM, ROW_TILE), _BF16),
            jax.ShapeDtypeStruct((s, CONV_WIDTH), _F32),
        ],
        compiler_params=pltpu.CompilerParams(
            dimension_semantics=("arbitrary",), vmem_limit_bytes=VMEM_LIMIT_BYTES),
        name="in_proj",
    )(x, pos_col, g, invf, wq, wk, wvt, wcg)


def _attn_kernel(ids_ref, nfull_ref, nneed_ref,
                 q_ref, k_ref, vt_ref, posq_ref, posk_ref,
                 lq1_ref, lk1_ref, lq2_ref, lk2_ref, g_ref,
                 o_ref,
                 qz_ref, s_ref, mblk_ref, m_ref, l_ref, acc_ref, *, lambda_init):
    qi = pl.program_id(1)
    n_kv = posk_ref.shape[0]
    tq = q_ref.shape[0]

    row = lax.broadcasted_iota(jnp.int32, (V_DIM, tq), 0)
    for h in range(HEAD_GROUP):
        qt = q_ref[:, h * V_DIM:(h + 1) * V_DIM].astype(_F32).T
        qz_ref[h, :, 0:tq] = jnp.where(row < HEAD_DIM, qt, 0.0).astype(_BF16)
        qz_ref[h, :, tq:2 * tq] = jnp.where(row < HEAD_DIM, 0.0, qt).astype(_BF16)
    m_ref[...] = jnp.full(m_ref.shape, MASK_VALUE, _F32)
    l_ref[...] = jnp.zeros(l_ref.shape, _F32)
    acc_ref[...] = jnp.zeros(acc_ref.shape, _F32)

    n_full = nfull_ref[qi]
    n_need = nneed_ref[qi]
    base = qi * n_kv
    chunks_per_map = tq // SCORE_CHUNK

    def visibility(kj):
        pk = jnp.broadcast_to(posk_ref[kj], (LANES, KV_TILE))
        pk = jnp.tile(pk.T, (1, SCORE_CHUNK // LANES))
        return [posq_ref[:, c * SCORE_CHUNK:(c + 1) * SCORE_CHUNK] >= pk
                for c in range(chunks_per_map)]

    def score_chunk(h, c, kj, slot, visible):
        rows = pl.ds(pl.multiple_of(kj * KV_TILE, KV_TILE), KV_TILE)
        cols = slice(c * SCORE_CHUNK, (c + 1) * SCORE_CHUNK)
        s = jnp.dot(k_ref[rows, h * V_DIM:(h + 1) * V_DIM], qz_ref[h, :, cols],
                    preferred_element_type=_F32)
        if visible is not None:
            s = jnp.where(visible[c % chunks_per_map], s, MASK_VALUE)
        mblk_ref[slot, h, :, cols] = jnp.max(s, axis=0, keepdims=True)
        s_ref[slot, h, :, cols] = s

    def value_chunk(h, c, kj, slot):
        cols = slice(c * SCORE_CHUNK, (c + 1) * SCORE_CHUNK)
        m_prev = m_ref[h, :, cols]
        m_new = jnp.maximum(m_prev, mblk_ref[slot, h, :, cols])
        alpha = jnp.exp2(m_prev - m_new)
        p = jnp.exp2(s_ref[slot, h, :, cols] - m_new)
        l_ref[h, :, cols] = alpha * l_ref[h, :, cols] + jnp.sum(p, axis=0, keepdims=True)
        acc_ref[h, :, cols] = alpha * acc_ref[h, :, cols] + jnp.dot(
            vt_ref[h, kj], p.astype(_BF16), preferred_element_type=_F32)
        m_ref[h, :, cols] = m_new

    def stage(score_j, score_slot, masked, value_j, value_slot):
        if score_j is not None:
            score_kj = ids_ref[base + score_j]
            visible = visibility(score_kj) if masked else None
        if value_j is not None:
            value_kj = ids_ref[base + value_j]
        for h in range(HEAD_GROUP):
            for c in range(2 * chunks_per_map):
                if score_j is not None:
                    score_chunk(h, c, score_kj, score_slot, visible)
                if value_j is not None:
                    value_chunk(h, c, value_kj, value_slot)

    @pl.when(n_full > 0)
    def _():
        stage(0, 0, False, None, None)

    @pl.when(n_full == 0)
    def _():
        stage(0, 0, True, None, None)

    def body(j, carry):
        next_masked = j + 1 >= n_full
        for slot in range(2):
            for masked in (False, True):
                @pl.when(jnp.logical_and((j & 1) == slot, next_masked == masked))
                def _():
                    stage(j + 1, 1 - slot, masked, j, slot)
        return carry

    lax.fori_loop(0, n_need - 1, body, 0)

    last = n_need - 1
    for slot in range(2):
        @pl.when((last & 1) == slot)
        def _():
            stage(None, None, False, last, slot)

    lam = (jnp.exp(jnp.sum(lq1_ref[...] * lk1_ref[...], keepdims=True))
           - jnp.exp(jnp.sum(lq2_ref[...] * lk2_ref[...], keepdims=True))
           + lambda_init)
    for h in range(HEAD_GROUP):
        o = acc_ref[h] / l_ref[h]
        o = o[:, 0:tq] - lam * o[:, tq:2 * tq]
        o = o * lax.rsqrt(jnp.mean(o * o, axis=0, keepdims=True) + EPS)
        o = (o * g_ref[...]) * (1.0 - lambda_init)
        o_ref[:, h * V_DIM:(h + 1) * V_DIM] = o.T.astype(o_ref.dtype)


def _attention(q, k, vt, pos_row, lq1, lk1, lq2, lk2, g_col, lambda_init):
    s = q.shape[0]
    n_q = s // Q_TILE
    n_kv = s // KV_TILE
    pos = pos_row[0]
    qmin = pos.reshape(n_q, Q_TILE).min(axis=1)
    qmax = pos.reshape(n_q, Q_TILE).max(axis=1)
    kmin = pos.reshape(n_kv, KV_TILE).min(axis=1)
    kmax = pos.reshape(n_kv, KV_TILE).max(axis=1)
    needed = kmin[None, :] <= qmax[:, None]
    full = jnp.logical_and(needed, kmax[None, :] <= qmin[:, None])
    partial = jnp.logical_and(needed, jnp.logical_not(full))
    n_full = full.sum(axis=1).astype(jnp.int32)
    n_need = needed.sum(axis=1).astype(jnp.int32)
    rank = jnp.where(full, jnp.cumsum(full, axis=1) - 1,
                     jnp.where(partial, n_full[:, None] + jnp.cumsum(partial, axis=1) - 1, n_kv))
    tile_id = jnp.arange(n_kv, dtype=jnp.int32)
    ids = jnp.sum(jnp.where(rank[:, :, None] == tile_id[None, None, :], tile_id[None, :, None], 0),
                  axis=1).astype(jnp.int32).reshape(-1)
    posk = pos_row.reshape(n_kv, 1, KV_TILE)
    small = lambda g, i, *_: (0, 0)
    resident = dict(pipeline_mode=pl.Buffered(1))
    group_width = HEAD_GROUP * V_DIM
    grid_spec = pltpu.PrefetchScalarGridSpec(
        num_scalar_prefetch=3,
        grid=(N_HEADS // HEAD_GROUP, n_q),
        in_specs=[
            pl.BlockSpec((Q_TILE, group_width), lambda g, i, *_: (i, g)),
            pl.BlockSpec((s, group_width), lambda g, i, *_: (0, g), **resident),
            pl.BlockSpec((HEAD_GROUP, n_kv, V_DIM, KV_TILE), lambda g, i, *_: (g, 0, 0, 0),
                         **resident),
            pl.BlockSpec((1, Q_TILE), lambda g, i, *_: (0, i)),
            pl.BlockSpec((n_kv, 1, KV_TILE), lambda g, i, *_: (0, 0, 0)),
            pl.BlockSpec((1, HEAD_DIM), small),
            pl.BlockSpec((1, HEAD_DIM), small),
            pl.BlockSpec((1, HEAD_DIM), small),
            pl.BlockSpec((1, HEAD_DIM), small),
            pl.BlockSpec((V_DIM, 1), small),
        ],
        out_specs=pl.BlockSpec((Q_TILE, group_width), lambda g, i, *_: (i, g)),
        scratch_shapes=[
            pltpu.VMEM((HEAD_GROUP, V_DIM, 2 * Q_TILE), _BF16),
            pltpu.VMEM((2, HEAD_GROUP, KV_TILE, 2 * Q_TILE), _F32),
            pltpu.VMEM((2, HEAD_GROUP, 1, 2 * Q_TILE), _F32),
            pltpu.VMEM((HEAD_GROUP, 1, 2 * Q_TILE), _F32),
            pltpu.VMEM((HEAD_GROUP, 1, 2 * Q_TILE), _F32),
            pltpu.VMEM((HEAD_GROUP, V_DIM, 2 * Q_TILE), _F32),
        ],
    )
    return pl.pallas_call(
        functools.partial(_attn_kernel, lambda_init=lambda_init),
        grid_spec=grid_spec,
        out_shape=jax.ShapeDtypeStruct((s, ATTN_WIDTH), _BF16),
        compiler_params=pltpu.CompilerParams(
            dimension_semantics=("arbitrary", "arbitrary"), vmem_limit_bytes=VMEM_LIMIT_BYTES),
        name="diff_attn",
    )(ids, n_full, n_need, q, k, vt, pos_row, posk, lq1, lk1, lq2, lk2, g_col)


def _mix_kernel(c_ref, chalo_ref, att_ref, x_ref, dww_ref, dwb_ref, lng_ref, lnb_ref,
                wo_ref, g_ref, o_ref, ext_ref, conv_ref, cact_ref):
    i = pl.program_id(0)
    tm = c_ref.shape[0]
    ext_ref[0:CONV_HALO] = jnp.where(i > 0, chalo_ref[...], 0.0)
    ext_ref[CONV_HALO:CONV_HALO + tm] = c_ref[...]

    ln_g = lng_ref[...]
    ln_b = lnb_ref[...]
    first_tap = CONV_HALO - (CONV_K - 1)
    n_sub = CONV_ROWS // SUBLANES
    sublane = lax.broadcasted_iota(jnp.int32, (n_sub, SUBLANES, LANES), 1)

    for r0 in range(0, tm, CONV_ROWS):
        for l0 in range(0, CONV_WIDTH, LANES):
            lanes = slice(l0, l0 + LANES)
            acc = jnp.broadcast_to(dwb_ref[:, lanes], (CONV_ROWS, LANES))
            for shift in range(SUBLANES):
                rows = CONV_ROWS if shift == 0 else CONV_ROWS + SUBLANES
                part = None
                for j in range(CONV_K):
                    offset = first_tap + j
                    if offset % SUBLANES != shift:
                        continue
                    start = r0 + offset - shift
                    term = dww_ref[j:j + 1, lanes] * ext_ref[start:start + rows, lanes]
                    part = term if part is None else part + term
                if shift:
                    rot = pltpu.roll(part.reshape(n_sub + 1, SUBLANES, LANES), SUBLANES - shift, 1)
                    part = jnp.where(sublane < SUBLANES - shift, rot[:n_sub], rot[1:])
                    part = part.reshape(CONV_ROWS, LANES)
                acc = acc + part
            conv_ref[r0:r0 + CONV_ROWS, lanes] = acc

        for n0 in range(r0, r0 + CONV_ROWS, NORM_ROWS):
            acc = conv_ref[n0:n0 + NORM_ROWS, :]
            mu = jnp.mean(acc, axis=-1, keepdims=True)
            d = acc - mu
            var = jnp.mean(d * d, axis=-1, keepdims=True)
            y = d * lax.rsqrt(var + EPS) * ln_g + ln_b
            cact_ref[n0:n0 + NORM_ROWS, :] = (y * jax.nn.sigmoid(y)).astype(cact_ref.dtype)

        rows = slice(r0, r0 + CONV_ROWS)
        mix = (jnp.dot(att_ref[rows, :], wo_ref[0:ATTN_WIDTH, :], preferred_element_type=_F32)
               + jnp.dot(cact_ref[rows, :], wo_ref[ATTN_WIDTH:, :], preferred_element_type=_F32))
        o_ref[rows, :] = x_ref[rows, :] + mix * _rms_scale(mix) * g_ref[...]


def _mix(c, att, x, dww, dwb, lng, lnb, wo, g):
    s = x.shape[0]
    n_tiles = s // ROW_TILE
    halo_blocks = ROW_TILE // CONV_HALO
    const = lambda i: (0, 0)
    row = lambda i: (i, 0)
    return pl.pallas_call(
        _mix_kernel,
        grid=(n_tiles,),
        in_specs=[
            pl.BlockSpec((ROW_TILE, CONV_WIDTH), row),
            pl.BlockSpec((CONV_HALO, CONV_WIDTH), lambda i: (jnp.maximum(i * halo_blocks - 1, 0), 0)),
            pl.BlockSpec((ROW_TILE, ATTN_WIDTH), row),
            pl.BlockSpec((ROW_TILE, D_MODEL), row),
            pl.BlockSpec((CONV_K, CONV_WIDTH), const),
            pl.BlockSpec((1, CONV_WIDTH), const),
            pl.BlockSpec((1, CONV_WIDTH), const),
            pl.BlockSpec((1, CONV_WIDTH), const),
            pl.BlockSpec((D_MODEL, D_MODEL), const),
            pl.BlockSpec((1, D_MODEL), const),
        ],
        out_specs=pl.BlockSpec((ROW_TILE, D_MODEL), row),
        out_shape=jax.ShapeDtypeStruct((s, D_MODEL), _F32),
        scratch_shapes=[
            pltpu.VMEM((CONV_HALO + ROW_TILE, CONV_WIDTH), _F32),
            pltpu.VMEM((ROW_TILE, CONV_WIDTH), _F32),
            pltpu.VMEM((ROW_TILE, CONV_WIDTH), _BF16),
        ],
        compiler_params=pltpu.CompilerParams(
            dimension_semantics=("arbitrary",), vmem_limit_bytes=VMEM_LIMIT_BYTES),
        name="conv_mix",
    )(c, c, att, x, dww, dwb, lng, lnb, wo, g)


def _ffn_kernel(x_ref, xhalo_ref, pre_g_ref, wup_ref, dww_ref, dwb_ref, wdn_ref, post_g_ref,
                o_ref, ug_ref, uu_ref, act_ref):
    i = pl.program_id(0)
    tm = x_ref.shape[0]
    x = x_ref[...]
    xe = jnp.concatenate([xhalo_ref[...], x], axis=0)
    h = (xe * _rms_scale(xe) * pre_g_ref[...]).astype(_BF16)

    def conv(u_ref, col):
        w = dww_ref[:, col:col + FF_CHUNK]
        acc = dwb_ref[:, col:col + FF_CHUNK] + w[2:3] * u_ref[FFN_HALO:FFN_HALO + tm, :]
        acc = acc + w[1:2] * u_ref[FFN_HALO - 1:FFN_HALO - 1 + tm, :]
        return acc + w[0:1] * u_ref[FFN_HALO - 2:FFN_HALO - 2 + tm, :]

    def store_u(u_ref, u):
        u_ref[0:FFN_HALO] = jnp.where(i > 0, u[0:FFN_HALO], 0.0)
        u_ref[FFN_HALO:] = u[FFN_HALO:]

    for c in range(D_FF // FF_CHUNK):
        gcol = c * FF_CHUNK
        ucol = D_FF + c * FF_CHUNK
        store_u(ug_ref, jnp.dot(h, wup_ref[:, gcol:gcol + FF_CHUNK], preferred_element_type=_F32))
        store_u(uu_ref, jnp.dot(h, wup_ref[:, ucol:ucol + FF_CHUNK], preferred_element_type=_F32))
        gate = conv(ug_ref, gcol)
        up = conv(uu_ref, ucol)
        act_ref[:, gcol:gcol + FF_CHUNK] = (jax.nn.gelu(gate, approximate=True) * up).astype(_BF16)

    y = jnp.dot(act_ref[...], wdn_ref[...], preferred_element_type=_F32)
    o_ref[...] = x + y * _rms_scale(y) * post_g_ref[...]


def _ffn(x, pre_g, wup, dww, dwb, wdn, post_g):
    s = x.shape[0]
    n_tiles = s // FFN_TILE
    halo_blocks = FFN_TILE // FFN_HALO
    const = lambda i: (0, 0)
    row = lambda i: (i, 0)
    resident = dict(pipeline_mode=pl.Buffered(1))
    return pl.pallas_call(
        _ffn_kernel,
        grid=(n_tiles,),
        in_specs=[
            pl.BlockSpec((FFN_TILE, D_MODEL), row),
            pl.BlockSpec((FFN_HALO, D_MODEL), lambda i: (jnp.maximum(i * halo_blocks - 1, 0), 0)),
            pl.BlockSpec((1, D_MODEL), const),
            pl.BlockSpec((D_MODEL, 2 * D_FF), const, **resident),
            pl.BlockSpec((FFN_CONV_K, 2 * D_FF), const),
            pl.BlockSpec((1, 2 * D_FF), const),
            pl.BlockSpec((D_FF, D_MODEL), const, **resident),
            pl.BlockSpec((1, D_MODEL), const),
        ],
        out_specs=pl.BlockSpec((FFN_TILE, D_MODEL), row),
        out_shape=jax.ShapeDtypeStruct((s, D_MODEL), _F32),
        scratch_shapes=[
            pltpu.VMEM((FFN_HALO + FFN_TILE, FF_CHUNK), _F32),
            pltpu.VMEM((FFN_HALO + FFN_TILE, FF_CHUNK), _F32),
            pltpu.VMEM((FFN_TILE, D_FF), _BF16),
        ],
        compiler_params=pltpu.CompilerParams(
            dimension_semantics=("arbitrary",), vmem_limit_bytes=VMEM_LIMIT_BYTES),
        name="conv_ffn",
    )(x, x, pre_g, wup, dww, dwb, wdn, post_g)


def kernel(x, positions, attn_pre_g, attn_post_g, w_in, lambda_q1, lambda_k1, lambda_q2, lambda_k2,
           subln_g, conv_dw_w, conv_dw_b, conv_ln_g, conv_ln_b, w_out, ffn_pre_g, ffn_post_g, w_up,
           ffn_dw_w, ffn_dw_b, w_down):
    bsz, s, d = x.shape
    depth = w_in.shape[0]
    assert bsz == 1 and d == D_MODEL
    assert s % ROW_TILE == 0 and s % KV_TILE == 0 and s % Q_TILE == 0 and ROW_TILE == KV_TILE

    pos_row = positions.reshape(1, s).astype(jnp.int32)
    pos_col = pos_row.reshape(s, 1)
    half = HEAD_DIM // 2
    inv_freq = ROPE_THETA ** (-jnp.arange(0, HEAD_DIM, 2, dtype=_F32) / HEAD_DIM)
    invf = jnp.tile(inv_freq, LANES // half).reshape(1, LANES)

    row = lambda v: v.reshape(1, -1).astype(_F32)
    xs = x[0]
    for l in range(depth):
        lambda_init = 0.8 - 0.6 * math.exp(-0.3 * l)
        w = w_in[l].astype(_BF16)
        wq = w[:, :ATTN_WIDTH]
        wk = w[:, ATTN_WIDTH:2 * ATTN_WIDTH]
        wvt = w[:, 2 * ATTN_WIDTH:3 * ATTN_WIDTH].T
        wcg = w[:, 3 * ATTN_WIDTH:]
        q, k, vt, c = _in_proj(xs, pos_col, row(attn_pre_g[l]), invf, wq, wk, wvt, wcg)
        att = _attention(q, k, vt, pos_row, row(lambda_q1[l]), row(lambda_k1[l]),
                         row(lambda_q2[l]), row(lambda_k2[l]),
                         subln_g[l].reshape(V_DIM, 1).astype(_F32), lambda_init)
        xs = _mix(c, att, xs, conv_dw_w[l].astype(_F32), row(conv_dw_b[l]), row(conv_ln_g[l]),
                  row(conv_ln_b[l]), w_out[l].astype(_BF16), row(attn_post_g[l]))
        xs = _ffn(xs, row(ffn_pre_g[l]), w_up[l].astype(_BF16), ffn_dw_w[l].astype(_F32),
                  row(ffn_dw_b[l]), w_down[l].astype(_BF16), row(ffn_post_g[l]))
    return xs[None]
```

```python
import functools
import math

import jax
import jax.numpy as jnp
from jax import lax
from jax.experimental import pallas as pl
from jax.experimental.pallas import tpu as pltpu

D_MODEL = 1024
ATTN_WIDTH = 512
CONV_WIDTH = 512
HEAD_DIM = 64
V_DIM = 2 * HEAD_DIM
N_HEADS = ATTN_WIDTH // V_DIM
CONV_K = 31
FFN_CONV_K = 3
D_FF = 2816
ROPE_THETA = 10000.0
EPS = 1e-6

LANES = 128
SUBLANES = 8
MXU_COLS = 256
VMEM_LIMIT_BYTES = 56 * 1024 * 1024

ROW_TILE = 512
Q_TILE = 512
HEAD_GROUP = 4
SCORE_CHUNK = 512
HOT_UNROLL = 4
KV_TILE = 512
CONV_HALO = 32
CONV_ROWS = 128
NORM_ROWS = 64
FFN_TILE = 512
FFN_HALO = 8
FF_CHUNK = 256

MASK_VALUE = -1e30

_BF16 = jnp.bfloat16
_F32 = jnp.float32


def _rms_scale(x):
    return lax.rsqrt(jnp.mean(x * x, axis=-1, keepdims=True) + EPS)


def _nt_dot(a, b):
    return lax.dot_general(a, b, (((1,), (1,)), ((), ())), preferred_element_type=_F32)


def _in_proj_kernel(x_ref, pos_ref, g_ref, invf_ref, wq_ref, wk_ref, wvt_ref, wcg_ref,
                    q_ref, k_ref, vt_ref, c_ref):
    x = x_ref[...]
    h = (x * _rms_scale(x) * g_ref[...]).astype(_BF16)

    cg = jnp.dot(h, wcg_ref[...], preferred_element_type=_F32)

    tm = pos_ref.shape[0]
    half = HEAD_DIM // 2
    groups = LANES // half
    rows_per_group = tm // groups
    pos = pos_ref[...].astype(_F32)
    lane_group = lax.broadcasted_iota(jnp.int32, (rows_per_group, LANES), 1) // half
    packed = jnp.zeros((rows_per_group, LANES), _F32)
    for g in range(groups):
        rows_g = pos[g * rows_per_group:(g + 1) * rows_per_group]
        packed = jnp.where(lane_group == g, rows_g, packed)
    ang = packed * invf_ref[...]
    cos_packed = jnp.cos(ang)
    sin_packed = jnp.sin(ang)

    def spread(table):
        parts = []
        for g in range(groups):
            own = jnp.where(lane_group == g, table, 0.0)
            rep = own
            for step in range(1, groups):
                rep = rep + pltpu.roll(own, step * half, 1)
            parts.append(rep)
        return jnp.concatenate(parts, axis=0)

    cos = spread(cos_packed)
    sin = spread(sin_packed)
    lane = lax.broadcasted_iota(jnp.int32, cos.shape, 1)
    first_half = (lane & half) == 0
    sin_first = jnp.where(first_half, -sin, 0.0)
    sin_second = jnp.where(first_half, 0.0, sin)

    def rope(t):
        from_right = pltpu.roll(t, LANES - HEAD_DIM // 2, 1)
        from_left = pltpu.roll(t, HEAD_DIM // 2, 1)
        return t * cos + from_right * sin_first + from_left * sin_second

    scale = HEAD_DIM ** -0.5 * math.log2(math.e)
    q = jnp.dot(h, wq_ref[...], preferred_element_type=_F32)
    k = jnp.dot(h, wk_ref[...], preferred_element_type=_F32)
    c_ref[...] = cg[:, :CONV_WIDTH] * jax.nn.sigmoid(cg[:, CONV_WIDTH:])
    vt = _nt_dot(wvt_ref[...], h)
    for j in range(N_HEADS):
        sl = slice(j * LANES, (j + 1) * LANES)
        q_ref[:, sl] = (rope(q[:, sl]) * scale).astype(_BF16)
        k_ref[:, sl] = rope(k[:, sl]).astype(_BF16)
    for j in range(N_HEADS):
        vt_ref[j, 0] = vt[j * V_DIM:(j + 1) * V_DIM, :].astype(_BF16)


def _in_proj(x, pos_col, g, invf, wq, wk, wvt, wcg):
    s = x.shape[0]
    n_tiles = s // ROW_TILE
    const = lambda i: (0, 0)
    return pl.pallas_call(
        _in_proj_kernel,
        grid=(n_tiles,),
        in_specs=[
            pl.BlockSpec((ROW_TILE, D_MODEL), lambda i: (i, 0)),
            pl.BlockSpec((ROW_TILE, 1), lambda i: (i, 0)),
            pl.BlockSpec((1, D_MODEL), const),
            pl.BlockSpec((1, LANES), const),
            pl.BlockSpec((D_MODEL, ATTN_WIDTH), const),
            pl.BlockSpec((D_MODEL, ATTN_WIDTH), const),
            pl.BlockSpec((ATTN_WIDTH, D_MODEL), const),
            pl.BlockSpec((D_MODEL, 2 * CONV_WIDTH), const),
        ],
        out_specs=[
            pl.BlockSpec((ROW_TILE, ATTN_WIDTH), lambda i: (i, 0)),
            pl.BlockSpec((ROW_TILE, ATTN_WIDTH), lambda i: (i, 0)),
            pl.BlockSpec((N_HEADS, 1, V_DIM, ROW_TILE), lambda i: (0, i, 0, 0)),
            pl.BlockSpec((ROW_TILE, CONV_WIDTH), lambda i: (i, 0)),
        ],
        out_shape=[
            jax.ShapeDtypeStruct((s, ATTN_WIDTH), _BF16),
            jax.ShapeDtypeStruct((s, ATTN_WIDTH), _BF16),
            jax.ShapeDtypeStruct((N_HEADS, n_tiles, V_DIM, ROW_TILE), _BF16),
            jax.ShapeDtypeStruct((s, CONV_WIDTH), _F32),
        ],
        compiler_params=pltpu.CompilerParams(
            dimension_semantics=("arbitrary",), vmem_limit_bytes=VMEM_LIMIT_BYTES),
        name="in_proj",
    )(x, pos_col, g, invf, wq, wk, wvt, wcg)


def _attn_kernel(ids_ref, nfull_ref, nneed_ref,
                 q_ref, k_ref, vt_ref, posq_ref, posk_ref,
                 lq1_ref, lk1_ref, lq2_ref, lk2_ref, g_ref,
                 o_ref,
                 qz_ref, s_ref, mblk_ref, m_ref, l_ref, acc_ref, *, lambda_init):
    qi = pl.program_id(1)
    n_kv = posk_ref.shape[0]
    tq = q_ref.shape[0]

    row = lax.broadcasted_iota(jnp.int32, (V_DIM, tq), 0)
    for h in range(HEAD_GROUP):
        qt = q_ref[:, h * V_DIM:(h + 1) * V_DIM].astype(_F32).T
        qz_ref[h, :, 0:tq] = jnp.where(row < HEAD_DIM, qt, 0.0).astype(_BF16)
        qz_ref[h, :, tq:2 * tq] = jnp.where(row < HEAD_DIM, 0.0, qt).astype(_BF16)
    m_ref[...] = jnp.full(m_ref.shape, MASK_VALUE, _F32)
    l_ref[...] = jnp.zeros(l_ref.shape, _F32)
    acc_ref[...] = jnp.zeros(acc_ref.shape, _F32)

    n_full = nfull_ref[qi]
    n_need = nneed_ref[qi]
    base = qi * n_kv
    chunks_per_map = tq // SCORE_CHUNK

    def visibility(kj):
        pk = jnp.broadcast_to(posk_ref[kj], (LANES, KV_TILE))
        pk = jnp.tile(pk.T, (1, SCORE_CHUNK // LANES))
        return [posq_ref[:, c * SCORE_CHUNK:(c + 1) * SCORE_CHUNK] >= pk
                for c in range(chunks_per_map)]

    def score_chunk(h, c, kj, slot, visible):
        rows = pl.ds(pl.multiple_of(kj * KV_TILE, KV_TILE), KV_TILE)
        cols = slice(c * SCORE_CHUNK, (c + 1) * SCORE_CHUNK)
        s = jnp.dot(k_ref[rows, h * V_DIM:(h + 1) * V_DIM], qz_ref[h, :, cols],
                    preferred_element_type=_F32)
        if visible is not None:
            s = jnp.where(visible[c % chunks_per_map], s, MASK_VALUE)
        mblk_ref[slot, h, :, cols] = jnp.max(s, axis=0, keepdims=True)
        s_ref[slot, h, :, cols] = s

    def value_chunk(h, c, kj, slot):
        cols = slice(c * SCORE_CHUNK, (c + 1) * SCORE_CHUNK)
        m_prev = m_ref[h, :, cols]
        m_new = jnp.maximum(m_prev, mblk_ref[slot, h, :, cols])
        alpha = jnp.exp2(m_prev - m_new)
        p = jnp.exp2(s_ref[slot, h, :, cols] - m_new)
        l_ref[h, :, cols] = alpha * l_ref[h, :, cols] + jnp.sum(p, axis=0, keepdims=True)
        acc_ref[h, :, cols] = alpha * acc_ref[h, :, cols] + jnp.dot(
            vt_ref[h, kj], p.astype(_BF16), preferred_element_type=_F32)
        m_ref[h, :, cols] = m_new

    def stage(score_j, score_slot, masked, value_j, value_slot):
        if score_j is not None:
            score_kj = ids_ref[base + score_j]
            visible = visibility(score_kj) if masked else None
        if value_j is not None:
            value_kj = ids_ref[base + value_j]
        for h in range(HEAD_GROUP):
            for c in range(2 * chunks_per_map):
                if score_j is not None:
                    score_chunk(h, c, score_kj, score_slot, visible)
                if value_j is not None:
                    value_chunk(h, c, value_kj, value_slot)

    @pl.when(n_full > 0)
    def _():
        stage(0, 0, False, None, None)

    @pl.when(n_full == 0)
    def _():
        stage(0, 0, True, None, None)

    def body(j, carry):
        next_masked = j + 1 >= n_full
        for slot in range(2):
            for masked in (False, True):
                @pl.when(jnp.logical_and((j & 1) == slot, next_masked == masked))
                def _():
                    stage(j + 1, 1 - slot, masked, j, slot)
        return carry

    def unrolled_body(t, carry):
        j = t * HOT_UNROLL
        for u in range(HOT_UNROLL):
            stage(j + u + 1, (u + 1) % 2, False, j + u, u % 2)
        return carry

    n_unrolled = jnp.maximum(n_full - 1, 0) // HOT_UNROLL
    lax.fori_loop(0, n_unrolled, unrolled_body, 0)
    lax.fori_loop(HOT_UNROLL * n_unrolled, n_need - 1, body, 0)

    last = n_need - 1
    for slot in range(2):
        @pl.when((last & 1) == slot)
        def _():
            stage(None, None, False, last, slot)

    lam = (jnp.exp(jnp.sum(lq1_ref[...] * lk1_ref[...], keepdims=True))
           - jnp.exp(jnp.sum(lq2_ref[...] * lk2_ref[...], keepdims=True))
           + lambda_init)
    for h in range(HEAD_GROUP):
        o = acc_ref[h] / l_ref[h]
        o = o[:, 0:tq] - lam * o[:, tq:2 * tq]
        o = o * lax.rsqrt(jnp.mean(o * o, axis=0, keepdims=True) + EPS)
        o = (o * g_ref[...]) * (1.0 - lambda_init)
        o_ref[:, h * V_DIM:(h + 1) * V_DIM] = o.T.astype(o_ref.dtype)


def _attention(q, k, vt, pos_row, lq1, lk1, lq2, lk2, g_col, lambda_init):
    s = q.shape[0]
    n_q = s // Q_TILE
    n_kv = s // KV_TILE
    pos = pos_row[0]
    qmin = pos.reshape(n_q, Q_TILE).min(axis=1)
    qmax = pos.reshape(n_q, Q_TILE).max(axis=1)
    kmin = pos.reshape(n_kv, KV_TILE).min(axis=1)
    kmax = pos.reshape(n_kv, KV_TILE).max(axis=1)
    needed = kmin[None, :] <= qmax[:, None]
    full = jnp.logical_and(needed, kmax[None, :] <= qmin[:, None])
    partial = jnp.logical_and(needed, jnp.logical_not(full))
    n_full = full.sum(axis=1).astype(jnp.int32)
    n_need = needed.sum(axis=1).astype(jnp.int32)
    rank = jnp.where(full, jnp.cumsum(full, axis=1) - 1,
                     jnp.where(partial, n_full[:, None] + jnp.cumsum(partial, axis=1) - 1, n_kv))
    tile_id = jnp.arange(n_kv, dtype=jnp.int32)
    ids = jnp.sum(jnp.where(rank[:, :, None] == tile_id[None, None, :], tile_id[None, :, None], 0),
                  axis=1).astype(jnp.int32).reshape(-1)
    posk = pos_row.reshape(n_kv, 1, KV_TILE)
    small = lambda g, i, *_: (0, 0)
    resident = dict(pipeline_mode=pl.Buffered(1))
    group_width = HEAD_GROUP * V_DIM
    grid_spec = pltpu.PrefetchScalarGridSpec(
        num_scalar_prefetch=3,
        grid=(N_HEADS // HEAD_GROUP, n_q),
        in_specs=[
            pl.BlockSpec((Q_TILE, group_width), lambda g, i, *_: (i, g)),
            pl.BlockSpec((s, group_width), lambda g, i, *_: (0, g), **resident),
            pl.BlockSpec((HEAD_GROUP, n_kv, V_DIM, KV_TILE), lambda g, i, *_: (g, 0, 0, 0),
                         **resident),
            pl.BlockSpec((1, Q_TILE), lambda g, i, *_: (0, i)),
            pl.BlockSpec((n_kv, 1, KV_TILE), lambda g, i, *_: (0, 0, 0)),
            pl.BlockSpec((1, HEAD_DIM), small),
            pl.BlockSpec((1, HEAD_DIM), small),
            pl.BlockSpec((1, HEAD_DIM), small),
            pl.BlockSpec((1, HEAD_DIM), small),
            pl.BlockSpec((V_DIM, 1), small),
        ],
        out_specs=pl.BlockSpec((Q_TILE, group_width), lambda g, i, *_: (i, g)),
        scratch_shapes=[
            pltpu.VMEM((HEAD_GROUP, V_DIM, 2 * Q_TILE), _BF16),
            pltpu.VMEM((2, HEAD_GROUP, KV_TILE, 2 * Q_TILE), _F32),
            pltpu.VMEM((2, HEAD_GROUP, 1, 2 * Q_TILE), _F32),
            pltpu.VMEM((HEAD_GROUP, 1, 2 * Q_TILE), _F32),
            pltpu.VMEM((HEAD_GROUP, 1, 2 * Q_TILE), _F32),
            pltpu.VMEM((HEAD_GROUP, V_DIM, 2 * Q_TILE), _F32),
        ],
    )
    return pl.pallas_call(
        functools.partial(_attn_kernel, lambda_init=lambda_init),
        grid_spec=grid_spec,
        out_shape=jax.ShapeDtypeStruct((s, ATTN_WIDTH), _BF16),
        compiler_params=pltpu.CompilerParams(
            dimension_semantics=("arbitrary", "arbitrary"), vmem_limit_bytes=VMEM_LIMIT_BYTES),
        name="diff_attn",
    )(ids, n_full, n_need, q, k, vt, pos_row, posk, lq1, lk1, lq2, lk2, g_col)


def _mix_kernel(c_ref, chalo_ref, att_ref, x_ref, dww_ref, dwb_ref, lng_ref, lnb_ref,
                wo_ref, g_ref, o_ref, ext_ref, conv_ref, cact_ref):
    i = pl.program_id(0)
    tm = c_ref.shape[0]
    ext_ref[0:CONV_HALO] = jnp.where(i > 0, chalo_ref[...], 0.0)
    ext_ref[CONV_HALO:CONV_HALO + tm] = c_ref[...]

    ln_g = lng_ref[...]
    ln_b = lnb_ref[...]
    first_tap = CONV_HALO - (CONV_K - 1)
    n_sub = CONV_ROWS // SUBLANES
    sublane = lax.broadcasted_iota(jnp.int32, (n_sub, SUBLANES, LANES), 1)

    for r0 in range(0, tm, CONV_ROWS):
        for l0 in range(0, CONV_WIDTH, LANES):
            lanes = slice(l0, l0 + LANES)
            acc = jnp.broadcast_to(dwb_ref[:, lanes], (CONV_ROWS, LANES))
            for shift in range(SUBLANES):
                rows = CONV_ROWS if shift == 0 else CONV_ROWS + SUBLANES
                part = None
                for j in range(CONV_K):
                    offset = first_tap + j
                    if offset % SUBLANES != shift:
                        continue
                    start = r0 + offset - shift
                    term = dww_ref[j:j + 1, lanes] * ext_ref[start:start + rows, lanes]
                    part = term if part is None else part + term
                if shift:
                    rot = pltpu.roll(part.reshape(n_sub + 1, SUBLANES, LANES), SUBLANES - shift, 1)
                    part = jnp.where(sublane < SUBLANES - shift, rot[:n_sub], rot[1:])
                    part = part.reshape(CONV_ROWS, LANES)
                acc = acc + part
            conv_ref[r0:r0 + CONV_ROWS, lanes] = acc

        for n0 in range(r0, r0 + CONV_ROWS, NORM_ROWS):
            acc = conv_ref[n0:n0 + NORM_ROWS, :]
            mu = jnp.mean(acc, axis=-1, keepdims=True)
            d = acc - mu
            var = jnp.mean(d * d, axis=-1, keepdims=True)
            y = d * lax.rsqrt(var + EPS) * ln_g + ln_b
            cact_ref[n0:n0 + NORM_ROWS, :] = (y * jax.nn.sigmoid(y)).astype(cact_ref.dtype)

        rows = slice(r0, r0 + CONV_ROWS)
        mix = (jnp.dot(att_ref[rows, :], wo_ref[0:ATTN_WIDTH, :], preferred_element_type=_F32)
               + jnp.dot(cact_ref[rows, :], wo_ref[ATTN_WIDTH:, :], preferred_element_type=_F32))
        o_ref[rows, :] = x_ref[rows, :] + mix * _rms_scale(mix) * g_ref[...]


def _mix(c, att, x, dww, dwb, lng, lnb, wo, g):
    s = x.shape[0]
    n_tiles = s // ROW_TILE
    halo_blocks = ROW_TILE // CONV_HALO
    const = lambda i: (0, 0)
    row = lambda i: (i, 0)
    return pl.pallas_call(
        _mix_kernel,
        grid=(n_tiles,),
        in_specs=[
            pl.BlockSpec((ROW_TILE, CONV_WIDTH), row),
            pl.BlockSpec((CONV_HALO, CONV_WIDTH), lambda i: (jnp.maximum(i * halo_blocks - 1, 0), 0)),
            pl.BlockSpec((ROW_TILE, ATTN_WIDTH), row),
            pl.BlockSpec((ROW_TILE, D_MODEL), row),
            pl.BlockSpec((CONV_K, CONV_WIDTH), const),
            pl.BlockSpec((1, CONV_WIDTH), const),
            pl.BlockSpec((1, CONV_WIDTH), const),
            pl.BlockSpec((1, CONV_WIDTH), const),
            pl.BlockSpec((D_MODEL, D_MODEL), const),
            pl.BlockSpec((1, D_MODEL), const),
        ],
        out_specs=pl.BlockSpec((ROW_TILE, D_MODEL), row),
        out_shape=jax.ShapeDtypeStruct((s, D_MODEL), _F32),
        scratch_shapes=[
            pltpu.VMEM((CONV_HALO + ROW_TILE, CONV_WIDTH), _F32),
            pltpu.VMEM((ROW_TILE, CONV_WIDTH), _F32),
            pltpu.VMEM((ROW_TILE, CONV_WIDTH), _BF16),
        ],
        compiler_params=pltpu.CompilerParams(
            dimension_semantics=("arbitrary",), vmem_limit_bytes=VMEM_LIMIT_BYTES),
        name="conv_mix",
    )(c, c, att, x, dww, dwb, lng, lnb, wo, g)


def _ffn_kernel(x_ref, xhalo_ref, pre_g_ref, wup_ref, dww_ref, dwb_ref, wdn_ref, post_g_ref,
                o_ref, ug_ref, uu_ref, act_ref):
    i = pl.program_id(0)
    tm = x_ref.shape[0]
    x = x_ref[...]
    xe = jnp.concatenate([xhalo_ref[...], x], axis=0)
    h = (xe * _rms_scale(xe) * pre_g_ref[...]).astype(_BF16)

    def conv(u_ref, col):
        w = dww_ref[:, col:col + FF_CHUNK]
        acc = dwb_ref[:, col:col + FF_CHUNK] + w[2:3] * u_ref[FFN_HALO:FFN_HALO + tm, :]
        acc = acc + w[1:2] * u_ref[FFN_HALO - 1:FFN_HALO - 1 + tm, :]
        return acc + w[0:1] * u_ref[FFN_HALO - 2:FFN_HALO - 2 + tm, :]

    def store_u(u_ref, u):
        u_ref[0:FFN_HALO] = jnp.where(i > 0, u[0:FFN_HALO], 0.0)
        u_ref[FFN_HALO:] = u[FFN_HALO:]

    for c in range(D_FF // FF_CHUNK):
        gcol = c * FF_CHUNK
        ucol = D_FF + c * FF_CHUNK
        store_u(ug_ref, jnp.dot(h, wup_ref[:, gcol:gcol + FF_CHUNK], preferred_element_type=_F32))
        store_u(uu_ref, jnp.dot(h, wup_ref[:, ucol:ucol + FF_CHUNK], preferred_element_type=_F32))
        gate = conv(ug_ref, gcol)
        up = conv(uu_ref, ucol)
        act_ref[:, gcol:gcol + FF_CHUNK] = (jax.nn.gelu(gate, approximate=True) * up).astype(_BF16)

    y = jnp.dot(act_ref[...], wdn_ref[...], preferred_element_type=_F32)
    o_ref[...] = x + y * _rms_scale(y) * post_g_ref[...]


def _ffn(x, pre_g, wup, dww, dwb, wdn, post_g):
    s = x.shape[0]
    n_tiles = s // FFN_TILE
    halo_blocks = FFN_TILE // FFN_HALO
    const = lambda i: (0, 0)
    row = lambda i: (i, 0)
    resident = dict(pipeline_mode=pl.Buffered(1))
    return pl.pallas_call(
        _ffn_kernel,
        grid=(n_tiles,),
        in_specs=[
            pl.BlockSpec((FFN_TILE, D_MODEL), row),
            pl.BlockSpec((FFN_HALO, D_MODEL), lambda i: (jnp.maximum(i * halo_blocks - 1, 0), 0)),
            pl.BlockSpec((1, D_MODEL), const),
            pl.BlockSpec((D_MODEL, 2 * D_FF), const, **resident),
            pl.BlockSpec((FFN_CONV_K, 2 * D_FF), const),
            pl.BlockSpec((1, 2 * D_FF), const),
            pl.BlockSpec((D_FF, D_MODEL), const, **resident),
            pl.BlockSpec((1, D_MODEL), const),
        ],
        out_specs=pl.BlockSpec((FFN_TILE, D_MODEL), row),
        out_shape=jax.ShapeDtypeStruct((s, D_MODEL), _F32),
        scratch_shapes=[
            pltpu.VMEM((FFN_HALO + FFN_TILE, FF_CHUNK), _F32),
            pltpu.VMEM((FFN_HALO + FFN_TILE, FF_CHUNK), _F32),
            pltpu.VMEM((FFN_TILE, D_FF), _BF16),
        ],
        compiler_params=pltpu.CompilerParams(
            dimension_semantics=("arbitrary",), vmem_limit_bytes=VMEM_LIMIT_BYTES),
        name="conv_ffn",
    )(x, x, pre_g, wup, dww, dwb, wdn, post_g)


def kernel(x, positions, attn_pre_g, attn_post_g, w_in, lambda_q1, lambda_k1, lambda_q2, lambda_k2,
           subln_g, conv_dw_w, conv_dw_b, conv_ln_g, conv_ln_b, w_out, ffn_pre_g, ffn_post_g, w_up,
           ffn_dw_w, ffn_dw_b, w_down):
    bsz, s, d = x.shape
    depth = w_in.shape[0]
    assert bsz == 1 and d == D_MODEL
    assert s % ROW_TILE == 0 and s % KV_TILE == 0 and s % Q_TILE == 0 and ROW_TILE == KV_TILE

    pos_row = positions.reshape(1, s).astype(jnp.int32)
    pos_col = pos_row.reshape(s, 1)
    half = HEAD_DIM // 2
    inv_freq = ROPE_THETA ** (-jnp.arange(0, HEAD_DIM, 2, dtype=_F32) / HEAD_DIM)
    invf = jnp.tile(inv_freq, LANES // half).reshape(1, LANES)

    row = lambda v: v.reshape(1, -1).astype(_F32)
    xs = x[0]
    for l in range(depth):
        lambda_init = 0.8 - 0.6 * math.exp(-0.3 * l)
        w = w_in[l].astype(_BF16)
        wq = w[:, :ATTN_WIDTH]
        wk = w[:, ATTN_WIDTH:2 * ATTN_WIDTH]
        wvt = w[:, 2 * ATTN_WIDTH:3 * ATTN_WIDTH].T
        wcg = w[:, 3 * ATTN_WIDTH:]
        q, k, vt, c = _in_proj(xs, pos_col, row(attn_pre_g[l]), invf, wq, wk, wvt, wcg)
        att = _attention(q, k, vt, pos_row, row(lambda_q1[l]), row(lambda_k1[l]),
                         row(lambda_q2[l]), row(lambda_k2[l]),
                         subln_g[l].reshape(V_DIM, 1).astype(_F32), lambda_init)
        xs = _mix(c, att, xs, conv_dw_w[l].astype(_F32), row(conv_dw_b[l]), row(conv_ln_g[l]),
                  row(conv_ln_b[l]), w_out[l].astype(_BF16), row(attn_post_g[l]))
        xs = _ffn(xs, row(ffn_pre_g[l]), w_up[l].astype(_BF16), ffn_dw_w[l].astype(_F32),
                  row(ffn_dw_b[l]), w_down[l].astype(_BF16), row(ffn_post_g[l]))
    return xs[None]
```

```python
import functools
import math

import jax
import jax.numpy as jnp
from jax import lax
from jax.experimental import pallas as pl
from jax.experimental.pallas import tpu as pltpu

D_MODEL = 1024
ATTN_WIDTH = 512
CONV_WIDTH = 512
HEAD_DIM = 64
V_DIM = 2 * HEAD_DIM
N_HEADS = ATTN_WIDTH // V_DIM
CONV_K = 31
FFN_CONV_K = 3
D_FF = 2816
ROPE_THETA = 10000.0
EPS = 1e-6

LANES = 128
SUBLANES = 8
MXU_COLS = 256
VMEM_LIMIT_BYTES = 56 * 1024 * 1024

ROW_TILE = 512
Q_TILE = 512
HEAD_GROUP = 4
SCORE_CHUNK = 512
HOT_UNROLL = 4
KV_TILE = 512
CONV_HALO = 32
CONV_ROWS = 128
NORM_ROWS = 64
FFN_TILE = 512
FFN_HALO = 8
FF_CHUNK = 256

MASK_VALUE = -1e30

_BF16 = jnp.bfloat16
_F32 = jnp.float32


def _rms_scale(x):
    return lax.rsqrt(jnp.mean(x * x, axis=-1, keepdims=True) + EPS)


def _nt_dot(a, b):
    return lax.dot_general(a, b, (((1,), (1,)), ((), ())), preferred_element_type=_F32)


def _in_proj_kernel(x_ref, pos_ref, g_ref, invf_ref, w_ref,
                    q_ref, k_ref, vt_ref, c_ref):
    x = x_ref[...]
    h = (x * _rms_scale(x) * g_ref[...]).astype(_BF16)
    wq_ref = w_ref.at[:, 0:ATTN_WIDTH]
    wk_ref = w_ref.at[:, ATTN_WIDTH:2 * ATTN_WIDTH]
    wv_ref = w_ref.at[:, 2 * ATTN_WIDTH:3 * ATTN_WIDTH]
    wcg_ref = w_ref.at[:, 3 * ATTN_WIDTH:3 * ATTN_WIDTH + 2 * CONV_WIDTH]

    cg = jnp.dot(h, wcg_ref[...], preferred_element_type=_F32)

    half = HEAD_DIM // 2
    groups = LANES // half
    rows_per_group = pos_ref.shape[2]
    pos = pos_ref[0].astype(_F32)
    by_group = jnp.concatenate(
        [jnp.broadcast_to(pos[g:g + 1], (half, rows_per_group)) for g in range(groups)], axis=0)
    packed = by_group.T
    lane_group = lax.broadcasted_iota(jnp.int32, (rows_per_group, LANES), 1) // half
    ang = packed * invf_ref[...]
    cos_packed = jnp.cos(ang)
    sin_packed = jnp.sin(ang)

    def spread(table):
        parts = []
        for g in range(groups):
            own = jnp.where(lane_group == g, table, 0.0)
            rep = own
            for step in range(1, groups):
                rep = rep + pltpu.roll(own, step * half, 1)
            parts.append(rep)
        return jnp.concatenate(parts, axis=0)

    cos = spread(cos_packed)
    sin = spread(sin_packed)
    lane = lax.broadcasted_iota(jnp.int32, cos.shape, 1)
    first_half = (lane & half) == 0
    sin_first = jnp.where(first_half, -sin, 0.0)
    sin_second = jnp.where(first_half, 0.0, sin)

    def rope(t):
        from_right = pltpu.roll(t, LANES - HEAD_DIM // 2, 1)
        from_left = pltpu.roll(t, HEAD_DIM // 2, 1)
        return t * cos + from_right * sin_first + from_left * sin_second

    scale = HEAD_DIM ** -0.5 * math.log2(math.e)
    q = jnp.dot(h, wq_ref[...], preferred_element_type=_F32)
    k = jnp.dot(h, wk_ref[...], preferred_element_type=_F32)
    c_ref[...] = cg[:, :CONV_WIDTH] * jax.nn.sigmoid(cg[:, CONV_WIDTH:])
    vt = jnp.dot(h, wv_ref[...], preferred_element_type=_F32).T
    for j in range(N_HEADS):
        sl = slice(j * LANES, (j + 1) * LANES)
        q_ref[:, sl] = (rope(q[:, sl]) * scale).astype(_BF16)
        k_ref[:, sl] = rope(k[:, sl]).astype(_BF16)
    for j in range(N_HEADS):
        vt_ref[j, 0] = vt[j * V_DIM:(j + 1) * V_DIM, :].astype(_BF16)


def _in_proj(x, pos_row, g, invf, w):
    s = x.shape[0]
    n_tiles = s // ROW_TILE
    groups = LANES // (HEAD_DIM // 2)
    assert ROW_TILE // groups == LANES
    pos_tiles = pos_row.reshape(n_tiles, groups, ROW_TILE // groups)
    const = lambda i: (0, 0)
    return pl.pallas_call(
        _in_proj_kernel,
        grid=(n_tiles,),
        in_specs=[
            pl.BlockSpec((ROW_TILE, D_MODEL), lambda i: (i, 0)),
            pl.BlockSpec((1, groups, ROW_TILE // groups), lambda i: (i, 0, 0)),
            pl.BlockSpec((1, D_MODEL), const),
            pl.BlockSpec((1, LANES), const),
            pl.BlockSpec((D_MODEL, 3 * ATTN_WIDTH + 2 * CONV_WIDTH), const),
        ],
        out_specs=[
            pl.BlockSpec((ROW_TILE, ATTN_WIDTH), lambda i: (i, 0)),
            pl.BlockSpec((ROW_TILE, ATTN_WIDTH), lambda i: (i, 0)),
            pl.BlockSpec((N_HEADS, 1, V_DIM, ROW_TILE), lambda i: (0, i, 0, 0)),
            pl.BlockSpec((ROW_TILE, CONV_WIDTH), lambda i: (i, 0)),
        ],
        out_shape=[
            jax.ShapeDtypeStruct((s, ATTN_WIDTH), _BF16),
            jax.ShapeDtypeStruct((s, ATTN_WIDTH), _BF16),
            jax.ShapeDtypeStruct((N_HEADS, n_tiles, V_DIM, ROW_TILE), _BF16),
            jax.ShapeDtypeStruct((s, CONV_WIDTH), _F32),
        ],
        compiler_params=pltpu.CompilerParams(
            dimension_semantics=("arbitrary",), vmem_limit_bytes=VMEM_LIMIT_BYTES),
        name="in_proj",
    )(x, pos_tiles, g, invf, w)


def _attn_kernel(ids_ref, nfull_ref, nneed_ref,
                 q_ref, k_ref, vt_ref, posq_ref, posk_ref,
                 lq1_ref, lk1_ref, lq2_ref, lk2_ref, g_ref,
                 o_ref,
                 qz_ref, s_ref, mblk_ref, m_ref, l_ref, acc_ref, *, lambda_init):
    qi = pl.program_id(1)
    n_kv = posk_ref.shape[0]
    tq = q_ref.shape[0]

    row = lax.broadcasted_iota(jnp.int32, (V_DIM, tq), 0)
    for h in range(HEAD_GROUP):
        qt = q_ref[:, h * V_DIM:(h + 1) * V_DIM].astype(_F32).T
        qz_ref[h, :, 0:tq] = jnp.where(row < HEAD_DIM, qt, 0.0).astype(_BF16)
        qz_ref[h, :, tq:2 * tq] = jnp.where(row < HEAD_DIM, 0.0, qt).astype(_BF16)
    m_ref[...] = jnp.full(m_ref.shape, MASK_VALUE, _F32)
    l_ref[...] = jnp.zeros(l_ref.shape, _F32)
    acc_ref[...] = jnp.zeros(acc_ref.shape, _F32)

    n_full = nfull_ref[qi]
    n_need = nneed_ref[qi]
    base = qi * n_kv
    chunks_per_map = tq // SCORE_CHUNK

    def visibility(kj):
        pk = jnp.broadcast_to(posk_ref[kj], (LANES, KV_TILE))
        pk = jnp.tile(pk.T, (1, SCORE_CHUNK // LANES))
        return [posq_ref[:, c * SCORE_CHUNK:(c + 1) * SCORE_CHUNK] >= pk
                for c in range(chunks_per_map)]

    def score_chunk(h, c, kj, slot, visible):
        rows = pl.ds(pl.multiple_of(kj * KV_TILE, KV_TILE), KV_TILE)
        cols = slice(c * SCORE_CHUNK, (c + 1) * SCORE_CHUNK)
        s = jnp.dot(k_ref[rows, h * V_DIM:(h + 1) * V_DIM], qz_ref[h, :, cols],
                    preferred_element_type=_F32)
        if visible is not None:
            s = jnp.where(visible[c % chunks_per_map], s, MASK_VALUE)
        mblk_ref[slot, h, :, cols] = jnp.max(s, axis=0, keepdims=True)
        s_ref[slot, h, :, cols] = s

    def value_chunk(h, c, kj, slot):
        cols = slice(c * SCORE_CHUNK, (c + 1) * SCORE_CHUNK)
        m_prev = m_ref[h, :, cols]
        m_new = jnp.maximum(m_prev, mblk_ref[slot, h, :, cols])
        alpha = jnp.exp2(m_prev - m_new)
        p = jnp.exp2(s_ref[slot, h, :, cols] - m_new)
        l_ref[h, :, cols] = alpha * l_ref[h, :, cols] + jnp.sum(p, axis=0, keepdims=True)
        acc_ref[h, :, cols] = alpha * acc_ref[h, :, cols] + jnp.dot(
            vt_ref[h, kj], p.astype(_BF16), preferred_element_type=_F32)
        m_ref[h, :, cols] = m_new

    def stage(score_j, score_slot, masked, value_j, value_slot):
        if score_j is not None:
            score_kj = ids_ref[base + score_j]
            visible = visibility(score_kj) if masked else None
        if value_j is not None:
            value_kj = ids_ref[base + value_j]
        for h in range(HEAD_GROUP):
            for c in range(2 * chunks_per_map):
                if score_j is not None:
                    score_chunk(h, c, score_kj, score_slot, visible)
                if value_j is not None:
                    value_chunk(h, c, value_kj, value_slot)

    @pl.when(n_full > 0)
    def _():
        stage(0, 0, False, None, None)

    @pl.when(n_full == 0)
    def _():
        stage(0, 0, True, None, None)

    def body(j, carry):
        next_masked = j + 1 >= n_full
        for slot in range(2):
            for masked in (False, True):
                @pl.when(jnp.logical_and((j & 1) == slot, next_masked == masked))
                def _():
                    stage(j + 1, 1 - slot, masked, j, slot)
        return carry

    def unrolled_body(t, carry):
        j = t * HOT_UNROLL
        for u in range(HOT_UNROLL):
            stage(j + u + 1, (u + 1) % 2, False, j + u, u % 2)
        return carry

    n_unrolled = jnp.maximum(n_full - 1, 0) // HOT_UNROLL
    lax.fori_loop(0, n_unrolled, unrolled_body, 0)
    lax.fori_loop(HOT_UNROLL * n_unrolled, n_need - 1, body, 0)

    last = n_need - 1
    for slot in range(2):
        @pl.when((last & 1) == slot)
        def _():
            stage(None, None, False, last, slot)

    lam = (jnp.exp(jnp.sum(lq1_ref[...] * lk1_ref[...], keepdims=True))
           - jnp.exp(jnp.sum(lq2_ref[...] * lk2_ref[...], keepdims=True))
           + lambda_init)
    for h in range(HEAD_GROUP):
        o = acc_ref[h] / l_ref[h]
        o = o[:, 0:tq] - lam * o[:, tq:2 * tq]
        o = o * lax.rsqrt(jnp.mean(o * o, axis=0, keepdims=True) + EPS)
        o = (o * g_ref[...]) * (1.0 - lambda_init)
        o_ref[:, h * V_DIM:(h + 1) * V_DIM] = o.T.astype(o_ref.dtype)


def _attention(q, k, vt, pos_row, lq1, lk1, lq2, lk2, g_col, lambda_init):
    s = q.shape[0]
    n_q = s // Q_TILE
    n_kv = s // KV_TILE
    pos = pos_row[0]
    qmin = pos.reshape(n_q, Q_TILE).min(axis=1)
    qmax = pos.reshape(n_q, Q_TILE).max(axis=1)
    kmin = pos.reshape(n_kv, KV_TILE).min(axis=1)
    kmax = pos.reshape(n_kv, KV_TILE).max(axis=1)
    needed = kmin[None, :] <= qmax[:, None]
    full = jnp.logical_and(needed, kmax[None, :] <= qmin[:, None])
    partial = jnp.logical_and(needed, jnp.logical_not(full))
    n_full = full.sum(axis=1).astype(jnp.int32)
    n_need = needed.sum(axis=1).astype(jnp.int32)
    rank = jnp.where(full, jnp.cumsum(full, axis=1) - 1,
                     jnp.where(partial, n_full[:, None] + jnp.cumsum(partial, axis=1) - 1, n_kv))
    tile_id = jnp.arange(n_kv, dtype=jnp.int32)
    ids = jnp.sum(jnp.where(rank[:, :, None] == tile_id[None, None, :], tile_id[None, :, None], 0),
                  axis=1).astype(jnp.int32).reshape(-1)
    posk = pos_row.reshape(n_kv, 1, KV_TILE)
    small = lambda g, i, *_: (0, 0)
    resident = dict(pipeline_mode=pl.Buffered(1))
    group_width = HEAD_GROUP * V_DIM
    grid_spec = pltpu.PrefetchScalarGridSpec(
        num_scalar_prefetch=3,
        grid=(N_HEADS // HEAD_GROUP, n_q),
        in_specs=[
            pl.BlockSpec((Q_TILE, group_width), lambda g, i, *_: (i, g)),
            pl.BlockSpec((s, group_width), lambda g, i, *_: (0, g), **resident),
            pl.BlockSpec((HEAD_GROUP, n_kv, V_DIM, KV_TILE), lambda g, i, *_: (g, 0, 0, 0),
                         **resident),
            pl.BlockSpec((1, Q_TILE), lambda g, i, *_: (0, i)),
            pl.BlockSpec((n_kv, 1, KV_TILE), lambda g, i, *_: (0, 0, 0)),
            pl.BlockSpec((1, HEAD_DIM), small),
            pl.BlockSpec((1, HEAD_DIM), small),
            pl.BlockSpec((1, HEAD_DIM), small),
            pl.BlockSpec((1, HEAD_DIM), small),
            pl.BlockSpec((V_DIM, 1), small),
        ],
        out_specs=pl.BlockSpec((Q_TILE, group_width), lambda g, i, *_: (i, g)),
        scratch_shapes=[
            pltpu.VMEM((HEAD_GROUP, V_DIM, 2 * Q_TILE), _BF16),
            pltpu.VMEM((2, HEAD_GROUP, KV_TILE, 2 * Q_TILE), _F32),
            pltpu.VMEM((2, HEAD_GROUP, 1, 2 * Q_TILE), _F32),
            pltpu.VMEM((HEAD_GROUP, 1, 2 * Q_TILE), _F32),
            pltpu.VMEM((HEAD_GROUP, 1, 2 * Q_TILE), _F32),
            pltpu.VMEM((HEAD_GROUP, V_DIM, 2 * Q_TILE), _F32),
        ],
    )
    return pl.pallas_call(
        functools.partial(_attn_kernel, lambda_init=lambda_init),
        grid_spec=grid_spec,
        out_shape=jax.ShapeDtypeStruct((s, ATTN_WIDTH), _BF16),
        compiler_params=pltpu.CompilerParams(
            dimension_semantics=("arbitrary", "arbitrary"), vmem_limit_bytes=VMEM_LIMIT_BYTES),
        name="diff_attn",
    )(ids, n_full, n_need, q, k, vt, pos_row, posk, lq1, lk1, lq2, lk2, g_col)


def _mix_kernel(c_ref, chalo_ref, att_ref, x_ref, dww_ref, dwb_ref, lng_ref, lnb_ref,
                wo_ref, g_ref, o_ref, ext_ref, conv_ref, cact_ref):
    i = pl.program_id(0)
    tm = c_ref.shape[0]
    ext_ref[0:CONV_HALO] = jnp.where(i > 0, chalo_ref[...], 0.0)
    ext_ref[CONV_HALO:CONV_HALO + tm] = c_ref[...]

    ln_g = lng_ref[...]
    ln_b = lnb_ref[...]
    first_tap = CONV_HALO - (CONV_K - 1)
    n_sub = CONV_ROWS // SUBLANES
    sublane = lax.broadcasted_iota(jnp.int32, (n_sub, SUBLANES, LANES), 1)

    for r0 in range(0, tm, CONV_ROWS):
        for l0 in range(0, CONV_WIDTH, LANES):
            lanes = slice(l0, l0 + LANES)
            acc = jnp.broadcast_to(dwb_ref[:, lanes], (CONV_ROWS, LANES))
            for shift in range(SUBLANES):
                rows = CONV_ROWS if shift == 0 else CONV_ROWS + SUBLANES
                part = None
                for j in range(CONV_K):
                    offset = first_tap + j
                    if offset % SUBLANES != shift:
                        continue
                    start = r0 + offset - shift
                    term = dww_ref[j:j + 1, lanes] * ext_ref[start:start + rows, lanes]
                    part = term if part is None else part + term
                if shift:
                    rot = pltpu.roll(part.reshape(n_sub + 1, SUBLANES, LANES), SUBLANES - shift, 1)
                    part = jnp.where(sublane < SUBLANES - shift, rot[:n_sub], rot[1:])
                    part = part.reshape(CONV_ROWS, LANES)
                acc = acc + part
            conv_ref[r0:r0 + CONV_ROWS, lanes] = acc

        for n0 in range(r0, r0 + CONV_ROWS, NORM_ROWS):
            acc = conv_ref[n0:n0 + NORM_ROWS, :]
            mu = jnp.mean(acc, axis=-1, keepdims=True)
            d = acc - mu
            var = jnp.mean(d * d, axis=-1, keepdims=True)
            y = d * lax.rsqrt(var + EPS) * ln_g + ln_b
            cact_ref[n0:n0 + NORM_ROWS, :] = (y * jax.nn.sigmoid(y)).astype(cact_ref.dtype)

        rows = slice(r0, r0 + CONV_ROWS)
        mix = (jnp.dot(att_ref[rows, :], wo_ref[0:ATTN_WIDTH, :], preferred_element_type=_F32)
               + jnp.dot(cact_ref[rows, :], wo_ref[ATTN_WIDTH:, :], preferred_element_type=_F32))
        o_ref[rows, :] = x_ref[rows, :] + mix * _rms_scale(mix) * g_ref[...]


def _mix(c, att, x, dww, dwb, lng, lnb, wo, g):
    s = x.shape[0]
    n_tiles = s // ROW_TILE
    halo_blocks = ROW_TILE // CONV_HALO
    const = lambda i: (0, 0)
    row = lambda i: (i, 0)
    return pl.pallas_call(
        _mix_kernel,
        grid=(n_tiles,),
        in_specs=[
            pl.BlockSpec((ROW_TILE, CONV_WIDTH), row),
            pl.BlockSpec((CONV_HALO, CONV_WIDTH), lambda i: (jnp.maximum(i * halo_blocks - 1, 0), 0)),
            pl.BlockSpec((ROW_TILE, ATTN_WIDTH), row),
            pl.BlockSpec((ROW_TILE, D_MODEL), row),
            pl.BlockSpec((CONV_K, CONV_WIDTH), const),
            pl.BlockSpec((1, CONV_WIDTH), const),
            pl.BlockSpec((1, CONV_WIDTH), const),
            pl.BlockSpec((1, CONV_WIDTH), const),
            pl.BlockSpec((D_MODEL, D_MODEL), const),
            pl.BlockSpec((1, D_MODEL), const),
        ],
        out_specs=pl.BlockSpec((ROW_TILE, D_MODEL), row),
        out_shape=jax.ShapeDtypeStruct((s, D_MODEL), _F32),
        scratch_shapes=[
            pltpu.VMEM((CONV_HALO + ROW_TILE, CONV_WIDTH), _F32),
            pltpu.VMEM((ROW_TILE, CONV_WIDTH), _F32),
            pltpu.VMEM((ROW_TILE, CONV_WIDTH), _BF16),
        ],
        compiler_params=pltpu.CompilerParams(
            dimension_semantics=("arbitrary",), vmem_limit_bytes=VMEM_LIMIT_BYTES),
        name="conv_mix",
    )(c, c, att, x, dww, dwb, lng, lnb, wo, g)


def _ffn_kernel(x_ref, xhalo_ref, pre_g_ref, wup_ref, dww_ref, dwb_ref, wdn_ref, post_g_ref,
                o_ref, ug_ref, uu_ref, act_ref):
    i = pl.program_id(0)
    tm = x_ref.shape[0]
    x = x_ref[...]
    xe = jnp.concatenate([xhalo_ref[...], x], axis=0)
    h = (xe * _rms_scale(xe) * pre_g_ref[...]).astype(_BF16)

    def conv(u_ref, col):
        w = dww_ref[:, col:col + FF_CHUNK]
        acc = dwb_ref[:, col:col + FF_CHUNK] + w[2:3] * u_ref[FFN_HALO:FFN_HALO + tm, :]
        acc = acc + w[1:2] * u_ref[FFN_HALO - 1:FFN_HALO - 1 + tm, :]
        return acc + w[0:1] * u_ref[FFN_HALO - 2:FFN_HALO - 2 + tm, :]

    def store_u(u_ref, u):
        u_ref[0:FFN_HALO] = jnp.where(i > 0, u[0:FFN_HALO], 0.0)
        u_ref[FFN_HALO:] = u[FFN_HALO:]

    for c in range(D_FF // FF_CHUNK):
        gcol = c * FF_CHUNK
        ucol = D_FF + c * FF_CHUNK
        store_u(ug_ref, jnp.dot(h, wup_ref[:, gcol:gcol + FF_CHUNK], preferred_element_type=_F32))
        store_u(uu_ref, jnp.dot(h, wup_ref[:, ucol:ucol + FF_CHUNK], preferred_element_type=_F32))
        gate = conv(ug_ref, gcol)
        up = conv(uu_ref, ucol)
        act_ref[:, gcol:gcol + FF_CHUNK] = (jax.nn.gelu(gate, approximate=True) * up).astype(_BF16)

    y = jnp.dot(act_ref[...], wdn_ref[...], preferred_element_type=_F32)
    o_ref[...] = x + y * _rms_scale(y) * post_g_ref[...]


def _ffn(x, pre_g, wup, dww, dwb, wdn, post_g):
    s = x.shape[0]
    n_tiles = s // FFN_TILE
    halo_blocks = FFN_TILE // FFN_HALO
    const = lambda i: (0, 0)
    row = lambda i: (i, 0)
    resident = dict(pipeline_mode=pl.Buffered(1))
    return pl.pallas_call(
        _ffn_kernel,
        grid=(n_tiles,),
        in_specs=[
            pl.BlockSpec((FFN_TILE, D_MODEL), row),
            pl.BlockSpec((FFN_HALO, D_MODEL), lambda i: (jnp.maximum(i * halo_blocks - 1, 0), 0)),
            pl.BlockSpec((1, D_MODEL), const),
            pl.BlockSpec((D_MODEL, 2 * D_FF), const, **resident),
            pl.BlockSpec((FFN_CONV_K, 2 * D_FF), const),
            pl.BlockSpec((1, 2 * D_FF), const),
            pl.BlockSpec((D_FF, D_MODEL), const, **resident),
            pl.BlockSpec((1, D_MODEL), const),
        ],
        out_specs=pl.BlockSpec((FFN_TILE, D_MODEL), row),
        out_shape=jax.ShapeDtypeStruct((s, D_MODEL), _F32),
        scratch_shapes=[
            pltpu.VMEM((FFN_HALO + FFN_TILE, FF_CHUNK), _F32),
            pltpu.VMEM((FFN_HALO + FFN_TILE, FF_CHUNK), _F32),
            pltpu.VMEM((FFN_TILE, D_FF), _BF16),
        ],
        compiler_params=pltpu.CompilerParams(
            dimension_semantics=("arbitrary",), vmem_limit_bytes=VMEM_LIMIT_BYTES),
        name="conv_ffn",
    )(x, x, pre_g, wup, dww, dwb, wdn, post_g)


def kernel(x, positions, attn_pre_g, attn_post_g, w_in, lambda_q1, lambda_k1, lambda_q2, lambda_k2,
           subln_g, conv_dw_w, conv_dw_b, conv_ln_g, conv_ln_b, w_out, ffn_pre_g, ffn_post_g, w_up,
           ffn_dw_w, ffn_dw_b, w_down):
    bsz, s, d = x.shape
    depth = w_in.shape[0]
    assert bsz == 1 and d == D_MODEL
    assert s % ROW_TILE == 0 and s % KV_TILE == 0 and s % Q_TILE == 0 and ROW_TILE == KV_TILE

    pos_row = positions.reshape(1, s).astype(jnp.int32)
    half = HEAD_DIM // 2
    inv_freq = ROPE_THETA ** (-jnp.arange(0, HEAD_DIM, 2, dtype=_F32) / HEAD_DIM)
    invf = jnp.tile(inv_freq, LANES // half).reshape(1, LANES)

    row = lambda v: v.reshape(1, -1).astype(_F32)
    xs = x[0]
    for l in range(depth):
        lambda_init = 0.8 - 0.6 * math.exp(-0.3 * l)
        q, k, vt, c = _in_proj(xs, pos_row, row(attn_pre_g[l]), invf, w_in[l].astype(_BF16))
        att = _attention(q, k, vt, pos_row, row(lambda_q1[l]), row(lambda_k1[l]),
                         row(lambda_q2[l]), row(lambda_k2[l]),
                         subln_g[l].reshape(V_DIM, 1).astype(_F32), lambda_init)
        xs = _mix(c, att, xs, conv_dw_w[l].astype(_F32), row(conv_dw_b[l]), row(conv_ln_g[l]),
                  row(conv_ln_b[l]), w_out[l].astype(_BF16), row(attn_post_g[l]))
        xs = _ffn(xs, row(ffn_pre_g[l]), w_up[l].astype(_BF16), ffn_dw_w[l].astype(_F32),
                  row(ffn_dw_b[l]), w_down[l].astype(_BF16), row(ffn_post_g[l]))
    return xs[None]
```

```python
import functools
import math

import jax
import jax.numpy as jnp
from jax import lax
from jax.experimental import pallas as pl
from jax.experimental.pallas import tpu as pltpu

D_MODEL = 1024
ATTN_WIDTH = 512
CONV_WIDTH = 512
HEAD_DIM = 64
V_DIM = 2 * HEAD_DIM
N_HEADS = ATTN_WIDTH // V_DIM
CONV_K = 31
FFN_CONV_K = 3
D_FF = 2816
ROPE_THETA = 10000.0
EPS = 1e-6

LANES = 128
SUBLANES = 8
MXU_COLS = 256
VMEM_LIMIT_BYTES = 56 * 1024 * 1024

ROW_TILE = 512
Q_TILE = 512
HEAD_GROUP = 4
SCORE_CHUNK = 512
HOT_UNROLL = 4
KV_TILE = 512
CONV_HALO = 32
CONV_ROWS = ROW_TILE
NORM_ROWS = 64
FFN_TILE = 512
FFN_HALO = 8
FF_CHUNK = MXU_COLS
STAGE_DEPTH = 4
UP_STAGE_ROWS = 128

MASK_VALUE = -1e30

_BF16 = jnp.bfloat16
_F32 = jnp.float32


def _rms_scale(x):
    return lax.rsqrt(jnp.mean(x * x, axis=-1, keepdims=True) + EPS)


def _in_proj_kernel(x_ref, pos_ref, g_ref, invf_ref, w_ref,
                    q_ref, k_ref, vt_ref, c_ref):
    x = x_ref[...]
    h = (x * _rms_scale(x) * g_ref[...]).astype(_BF16)
    wq_ref = w_ref.at[:, 0:ATTN_WIDTH]
    wk_ref = w_ref.at[:, ATTN_WIDTH:2 * ATTN_WIDTH]
    wv_ref = w_ref.at[:, 2 * ATTN_WIDTH:3 * ATTN_WIDTH]
    wcg_ref = w_ref.at[:, 3 * ATTN_WIDTH:3 * ATTN_WIDTH + 2 * CONV_WIDTH]

    cg = jnp.dot(h, wcg_ref[...], preferred_element_type=_F32)

    half = HEAD_DIM // 2
    groups = LANES // half
    rows_per_group = pos_ref.shape[2]
    pos = pos_ref[0].astype(_F32)
    by_group = jnp.concatenate(
        [jnp.broadcast_to(pos[g:g + 1], (half, rows_per_group)) for g in range(groups)], axis=0)
    packed = by_group.T
    lane_group = lax.broadcasted_iota(jnp.int32, (rows_per_group, LANES), 1) // half
    ang = packed * invf_ref[...]
    cos_packed = jnp.cos(ang)
    sin_packed = jnp.sin(ang)

    def spread(table):
        parts = []
        for g in range(groups):
            own = jnp.where(lane_group == g, table, 0.0)
            rep = own
            for step in range(1, groups):
                rep = rep + pltpu.roll(own, step * half, 1)
            parts.append(rep)
        return jnp.concatenate(parts, axis=0)

    cos = spread(cos_packed)
    sin = spread(sin_packed)
    lane = lax.broadcasted_iota(jnp.int32, cos.shape, 1)
    first_half = (lane & half) == 0
    sin_first = jnp.where(first_half, -sin, 0.0)
    sin_second = jnp.where(first_half, 0.0, sin)

    def rope(t):
        from_right = pltpu.roll(t, LANES - HEAD_DIM // 2, 1)
        from_left = pltpu.roll(t, HEAD_DIM // 2, 1)
        return t * cos + from_right * sin_first + from_left * sin_second

    scale = HEAD_DIM ** -0.5 * math.log2(math.e)
    q = jnp.dot(h, wq_ref[...], preferred_element_type=_F32)
    k = jnp.dot(h, wk_ref[...], preferred_element_type=_F32)
    c_ref[...] = cg[:, :CONV_WIDTH] * jax.nn.sigmoid(cg[:, CONV_WIDTH:])
    vt = jnp.dot(h, wv_ref[...], preferred_element_type=_F32).T
    for j in range(N_HEADS):
        sl = slice(j * LANES, (j + 1) * LANES)
        q_ref[:, sl] = (rope(q[:, sl]) * scale).astype(_BF16)
        k_ref[:, sl] = rope(k[:, sl]).astype(_BF16)
    for j in range(N_HEADS):
        vt_ref[j, 0] = vt[j * V_DIM:(j + 1) * V_DIM, :].astype(_BF16)


def _in_proj(x, pos_row, g, invf, w):
    s = x.shape[0]
    n_tiles = s // ROW_TILE
    groups = LANES // (HEAD_DIM // 2)
    assert ROW_TILE // groups == LANES
    pos_tiles = pos_row.reshape(n_tiles, groups, ROW_TILE // groups)
    const = lambda i: (0, 0)
    return pl.pallas_call(
        _in_proj_kernel,
        grid=(n_tiles,),
        in_specs=[
            pl.BlockSpec((ROW_TILE, D_MODEL), lambda i: (i, 0)),
            pl.BlockSpec((1, groups, ROW_TILE // groups), lambda i: (i, 0, 0)),
            pl.BlockSpec((1, D_MODEL), const),
            pl.BlockSpec((1, LANES), const),
            pl.BlockSpec((D_MODEL, 3 * ATTN_WIDTH + 2 * CONV_WIDTH), const),
        ],
        out_specs=[
            pl.BlockSpec((ROW_TILE, ATTN_WIDTH), lambda i: (i, 0)),
            pl.BlockSpec((ROW_TILE, ATTN_WIDTH), lambda i: (i, 0)),
            pl.BlockSpec((N_HEADS, 1, V_DIM, ROW_TILE), lambda i: (0, i, 0, 0)),
            pl.BlockSpec((ROW_TILE, CONV_WIDTH), lambda i: (i, 0)),
        ],
        out_shape=[
            jax.ShapeDtypeStruct((s, ATTN_WIDTH), _BF16),
            jax.ShapeDtypeStruct((s, ATTN_WIDTH), _BF16),
            jax.ShapeDtypeStruct((N_HEADS, n_tiles, V_DIM, ROW_TILE), _BF16),
            jax.ShapeDtypeStruct((s, CONV_WIDTH), _F32),
        ],
        compiler_params=pltpu.CompilerParams(
            dimension_semantics=("arbitrary",), vmem_limit_bytes=VMEM_LIMIT_BYTES),
        name="in_proj",
    )(x, pos_tiles, g, invf, w)


def _attn_kernel(ids_ref, nfull_ref, nneed_ref,
                 q_ref, k_ref, vt_ref, posq_ref, posk_ref,
                 lq1_ref, lk1_ref, lq2_ref, lk2_ref, g_ref,
                 o_ref,
                 qz_ref, s_ref, mblk_ref, m_ref, l_ref, acc_ref, *, lambda_init):
    qi = pl.program_id(1)
    n_kv = posk_ref.shape[0]
    tq = q_ref.shape[0]

    row = lax.broadcasted_iota(jnp.int32, (V_DIM, tq), 0)
    for h in range(HEAD_GROUP):
        qt = q_ref[:, h * V_DIM:(h + 1) * V_DIM].astype(_F32).T
        qz_ref[h, :, 0:tq] = jnp.where(row < HEAD_DIM, qt, 0.0).astype(_BF16)
        qz_ref[h, :, tq:2 * tq] = jnp.where(row < HEAD_DIM, 0.0, qt).astype(_BF16)
    m_ref[...] = jnp.full(m_ref.shape, MASK_VALUE, _F32)
    l_ref[...] = jnp.zeros(l_ref.shape, _F32)
    acc_ref[...] = jnp.zeros(acc_ref.shape, _F32)

    n_full = nfull_ref[qi]
    n_need = nneed_ref[qi]
    base = qi * n_kv
    chunks_per_map = tq // SCORE_CHUNK

    def visibility(kj):
        pk = jnp.broadcast_to(posk_ref[kj], (LANES, KV_TILE))
        pk = jnp.tile(pk.T, (1, SCORE_CHUNK // LANES))
        return [posq_ref[:, c * SCORE_CHUNK:(c + 1) * SCORE_CHUNK] >= pk
                for c in range(chunks_per_map)]

    def score_chunk(h, c, kj, slot, visible):
        rows = pl.ds(pl.multiple_of(kj * KV_TILE, KV_TILE), KV_TILE)
        cols = slice(c * SCORE_CHUNK, (c + 1) * SCORE_CHUNK)
        s = jnp.dot(k_ref[rows, h * V_DIM:(h + 1) * V_DIM], qz_ref[h, :, cols],
                    preferred_element_type=_F32)
        if visible is not None:
            s = jnp.where(visible[c % chunks_per_map], s, MASK_VALUE)
        mblk_ref[slot, h, :, cols] = jnp.max(s, axis=0, keepdims=True)
        s_ref[slot, h, :, cols] = s

    def value_chunk(h, c, kj, slot):
        cols = slice(c * SCORE_CHUNK, (c + 1) * SCORE_CHUNK)
        m_prev = m_ref[h, :, cols]
        m_new = jnp.maximum(m_prev, mblk_ref[slot, h, :, cols])
        alpha = jnp.exp2(m_prev - m_new)
        p = jnp.exp2(s_ref[slot, h, :, cols] - m_new)
        l_ref[h, :, cols] = alpha * l_ref[h, :, cols] + jnp.sum(p, axis=0, keepdims=True)
        acc_ref[h, :, cols] = alpha * acc_ref[h, :, cols] + jnp.dot(
            vt_ref[h, kj], p.astype(_BF16), preferred_element_type=_F32)
        m_ref[h, :, cols] = m_new

    def stage(score_j, score_slot, masked, value_j, value_slot):
        if score_j is not None:
            score_kj = ids_ref[base + score_j]
            visible = visibility(score_kj) if masked else None
        if value_j is not None:
            value_kj = ids_ref[base + value_j]
        for h in range(HEAD_GROUP):
            for c in range(2 * chunks_per_map):
                if score_j is not None:
                    score_chunk(h, c, score_kj, score_slot, visible)
                if value_j is not None:
                    value_chunk(h, c, value_kj, value_slot)

    @pl.when(n_full > 0)
    def _():
        stage(0, 0, False, None, None)

    @pl.when(n_full == 0)
    def _():
        stage(0, 0, True, None, None)

    def body(j, carry):
        next_masked = j + 1 >= n_full
        for slot in range(2):
            for masked in (False, True):
                @pl.when(jnp.logical_and((j & 1) == slot, next_masked == masked))
                def _():
                    stage(j + 1, 1 - slot, masked, j, slot)
        return carry

    def unrolled_body(t, carry):
        j = t * HOT_UNROLL
        for u in range(HOT_UNROLL):
            stage(j + u + 1, (u + 1) % 2, False, j + u, u % 2)
        return carry

    n_unrolled = jnp.maximum(n_full - 1, 0) // HOT_UNROLL
    lax.fori_loop(0, n_unrolled, unrolled_body, 0)
    lax.fori_loop(HOT_UNROLL * n_unrolled, n_need - 1, body, 0)

    last = n_need - 1
    for slot in range(2):
        @pl.when((last & 1) == slot)
        def _():
            stage(None, None, False, last, slot)

    lam = (jnp.exp(jnp.sum(lq1_ref[...] * lk1_ref[...], keepdims=True))
           - jnp.exp(jnp.sum(lq2_ref[...] * lk2_ref[...], keepdims=True))
           + lambda_init)
    for h in range(HEAD_GROUP):
        o = acc_ref[h] / l_ref[h]
        o = o[:, 0:tq] - lam * o[:, tq:2 * tq]
        o = o * lax.rsqrt(jnp.mean(o * o, axis=0, keepdims=True) + EPS)
        o = (o * g_ref[...]) * (1.0 - lambda_init)
        o_ref[:, h * V_DIM:(h + 1) * V_DIM] = o.T.astype(o_ref.dtype)


def _attention(q, k, vt, pos_row, lq1, lk1, lq2, lk2, g_col, lambda_init):
    s = q.shape[0]
    n_q = s // Q_TILE
    n_kv = s // KV_TILE
    pos = pos_row[0]
    qmin = pos.reshape(n_q, Q_TILE).min(axis=1)
    qmax = pos.reshape(n_q, Q_TILE).max(axis=1)
    kmin = pos.reshape(n_kv, KV_TILE).min(axis=1)
    kmax = pos.reshape(n_kv, KV_TILE).max(axis=1)
    needed = kmin[None, :] <= qmax[:, None]
    full = jnp.logical_and(needed, kmax[None, :] <= qmin[:, None])
    partial = jnp.logical_and(needed, jnp.logical_not(full))
    n_full = full.sum(axis=1).astype(jnp.int32)
    n_need = needed.sum(axis=1).astype(jnp.int32)
    rank = jnp.where(full, jnp.cumsum(full, axis=1) - 1,
                     jnp.where(partial, n_full[:, None] + jnp.cumsum(partial, axis=1) - 1, n_kv))
    tile_id = jnp.arange(n_kv, dtype=jnp.int32)
    ids = jnp.sum(jnp.where(rank[:, :, None] == tile_id[None, None, :], tile_id[None, :, None], 0),
                  axis=1).astype(jnp.int32).reshape(-1)
    posk = pos_row.reshape(n_kv, 1, KV_TILE)
    small = lambda g, i, *_: (0, 0)
    resident = dict(pipeline_mode=pl.Buffered(1))
    group_width = HEAD_GROUP * V_DIM
    grid_spec = pltpu.PrefetchScalarGridSpec(
        num_scalar_prefetch=3,
        grid=(N_HEADS // HEAD_GROUP, n_q),
        in_specs=[
            pl.BlockSpec((Q_TILE, group_width), lambda g, i, *_: (i, g)),
            pl.BlockSpec((s, group_width), lambda g, i, *_: (0, g), **resident),
            pl.BlockSpec((HEAD_GROUP, n_kv, V_DIM, KV_TILE), lambda g, i, *_: (g, 0, 0, 0),
                         **resident),
            pl.BlockSpec((1, Q_TILE), lambda g, i, *_: (0, i)),
            pl.BlockSpec((n_kv, 1, KV_TILE), lambda g, i, *_: (0, 0, 0)),
            pl.BlockSpec((1, HEAD_DIM), small),
            pl.BlockSpec((1, HEAD_DIM), small),
            pl.BlockSpec((1, HEAD_DIM), small),
            pl.BlockSpec((1, HEAD_DIM), small),
            pl.BlockSpec((V_DIM, 1), small),
        ],
        out_specs=pl.BlockSpec((Q_TILE, group_width), lambda g, i, *_: (i, g)),
        scratch_shapes=[
            pltpu.VMEM((HEAD_GROUP, V_DIM, 2 * Q_TILE), _BF16),
            pltpu.VMEM((2, HEAD_GROUP, KV_TILE, 2 * Q_TILE), _F32),
            pltpu.VMEM((2, HEAD_GROUP, 1, 2 * Q_TILE), _F32),
            pltpu.VMEM((HEAD_GROUP, 1, 2 * Q_TILE), _F32),
            pltpu.VMEM((HEAD_GROUP, 1, 2 * Q_TILE), _F32),
            pltpu.VMEM((HEAD_GROUP, V_DIM, 2 * Q_TILE), _F32),
        ],
    )
    return pl.pallas_call(
        functools.partial(_attn_kernel, lambda_init=lambda_init),
        grid_spec=grid_spec,
        out_shape=jax.ShapeDtypeStruct((s, ATTN_WIDTH), _BF16),
        compiler_params=pltpu.CompilerParams(
            dimension_semantics=("arbitrary", "arbitrary"), vmem_limit_bytes=VMEM_LIMIT_BYTES),
        name="diff_attn",
    )(ids, n_full, n_need, q, k, vt, pos_row, posk, lq1, lk1, lq2, lk2, g_col)


def _mix_kernel(c_ref, chalo_ref, att_ref, x_ref, dww_ref, dwb_ref, lng_ref, lnb_ref,
                wo_ref, g_ref, o_ref, ext_ref, conv_ref, cact_ref):
    i = pl.program_id(0)
    tm = c_ref.shape[0]
    ext_ref[0:CONV_HALO] = jnp.where(i > 0, chalo_ref[...], 0.0)
    ext_ref[CONV_HALO:CONV_HALO + tm] = c_ref[...]

    ln_g = lng_ref[...]
    ln_b = lnb_ref[...]
    first_tap = CONV_HALO - (CONV_K - 1)
    n_sub = CONV_ROWS // SUBLANES
    sublane = lax.broadcasted_iota(jnp.int32, (n_sub, SUBLANES, LANES), 1)

    for r0 in range(0, tm, CONV_ROWS):
        for l0 in range(0, CONV_WIDTH, LANES):
            lanes = slice(l0, l0 + LANES)
            acc = jnp.broadcast_to(dwb_ref[:, lanes], (CONV_ROWS, LANES))
            for shift in range(SUBLANES):
                rows = CONV_ROWS if shift == 0 else CONV_ROWS + SUBLANES
                part = None
                for j in range(CONV_K):
                    offset = first_tap + j
                    if offset % SUBLANES != shift:
                        continue
                    start = r0 + offset - shift
                    term = dww_ref[j:j + 1, lanes] * ext_ref[start:start + rows, lanes]
                    part = term if part is None else part + term
                if shift:
                    rot = pltpu.roll(part.reshape(n_sub + 1, SUBLANES, LANES), SUBLANES - shift, 1)
                    part = jnp.where(sublane < SUBLANES - shift, rot[:n_sub], rot[1:])
                    part = part.reshape(CONV_ROWS, LANES)
                acc = acc + part
            conv_ref[r0:r0 + CONV_ROWS, lanes] = acc

    for n0 in range(0, tm, NORM_ROWS):
        acc = conv_ref[n0:n0 + NORM_ROWS, :]
        mu = jnp.mean(acc, axis=-1, keepdims=True)
        d = acc - mu
        var = jnp.mean(d * d, axis=-1, keepdims=True)
        y = d * lax.rsqrt(var + EPS) * ln_g + ln_b
        cact_ref[n0:n0 + NORM_ROWS, :] = (y * jax.nn.sigmoid(y)).astype(cact_ref.dtype)

    mix = (jnp.dot(att_ref[...], wo_ref[0:ATTN_WIDTH, :], preferred_element_type=_F32)
           + jnp.dot(cact_ref[...], wo_ref[ATTN_WIDTH:, :], preferred_element_type=_F32))
    o_ref[...] = x_ref[...] + mix * _rms_scale(mix) * g_ref[...]


def _mix(c, att, x, dww, dwb, lng, lnb, wo, g):
    s = x.shape[0]
    n_tiles = s // ROW_TILE
    halo_blocks = ROW_TILE // CONV_HALO
    const = lambda i: (0, 0)
    row = lambda i: (i, 0)
    return pl.pallas_call(
        _mix_kernel,
        grid=(n_tiles,),
        in_specs=[
            pl.BlockSpec((ROW_TILE, CONV_WIDTH), row),
            pl.BlockSpec((CONV_HALO, CONV_WIDTH), lambda i: (jnp.maximum(i * halo_blocks - 1, 0), 0)),
            pl.BlockSpec((ROW_TILE, ATTN_WIDTH), row),
            pl.BlockSpec((ROW_TILE, D_MODEL), row),
            pl.BlockSpec((CONV_K, CONV_WIDTH), const),
            pl.BlockSpec((1, CONV_WIDTH), const),
            pl.BlockSpec((1, CONV_WIDTH), const),
            pl.BlockSpec((1, CONV_WIDTH), const),
            pl.BlockSpec((D_MODEL, D_MODEL), const),
            pl.BlockSpec((1, D_MODEL), const),
        ],
        out_specs=pl.BlockSpec((ROW_TILE, D_MODEL), row),
        out_shape=jax.ShapeDtypeStruct((s, D_MODEL), _F32),
        scratch_shapes=[
            pltpu.VMEM((CONV_HALO + ROW_TILE, CONV_WIDTH), _F32),
            pltpu.VMEM((ROW_TILE, CONV_WIDTH), _F32),
            pltpu.VMEM((ROW_TILE, CONV_WIDTH), _BF16),
        ],
        compiler_params=pltpu.CompilerParams(
            dimension_semantics=("arbitrary",), vmem_limit_bytes=VMEM_LIMIT_BYTES),
        name="conv_mix",
    )(c, c, att, x, dww, dwb, lng, lnb, wo, g)


def _stage_weight(chunk_copy, n_chunks, store_chunk):
    ahead = STAGE_DEPTH - 1
    for c in range(min(ahead, n_chunks)):
        chunk_copy(c).start()
    for c in range(n_chunks):
        if c + ahead < n_chunks:
            chunk_copy(c + ahead).start()
        chunk_copy(c).wait()
        store_chunk(c)


def _ffn_kernel(x_ref, xhalo_ref, pre_g_ref, wup_hbm, dww_ref, dwb_ref, wdn_hbm, post_g_ref,
                o_ref, ug_ref, uu_ref, act_ref, wup_ref, wdn_ref, up_stage, dn_stage, sem):
    i = pl.program_id(0)
    tm = x_ref.shape[0]

    @pl.when(i == 0)
    def _():
        def up_copy(c):
            slot = c % STAGE_DEPTH
            return pltpu.make_async_copy(
                wup_hbm.at[c * UP_STAGE_ROWS:(c + 1) * UP_STAGE_ROWS, :], up_stage.at[slot],
                sem.at[slot])

        def up_store(c):
            wup_ref[c * UP_STAGE_ROWS:(c + 1) * UP_STAGE_ROWS, :] = (
                up_stage[c % STAGE_DEPTH].astype(_BF16))

        _stage_weight(up_copy, D_MODEL // UP_STAGE_ROWS, up_store)

        def dn_copy(c):
            slot = c % STAGE_DEPTH
            return pltpu.make_async_copy(
                wdn_hbm.at[c * FF_CHUNK:(c + 1) * FF_CHUNK, :], dn_stage.at[slot], sem.at[slot])

        def dn_store(c):
            wdn_ref[c * FF_CHUNK:(c + 1) * FF_CHUNK, :] = dn_stage[c % STAGE_DEPTH].astype(_BF16)

        _stage_weight(dn_copy, D_FF // FF_CHUNK, dn_store)

    x = x_ref[...]
    xe = jnp.concatenate([xhalo_ref[...], x], axis=0)
    h = (xe * _rms_scale(xe) * pre_g_ref[...]).astype(_BF16)

    def conv(u_ref, col):
        w = dww_ref[:, col:col + FF_CHUNK]
        acc = dwb_ref[:, col:col + FF_CHUNK] + w[2:3] * u_ref[FFN_HALO:FFN_HALO + tm, :]
        acc = acc + w[1:2] * u_ref[FFN_HALO - 1:FFN_HALO - 1 + tm, :]
        return acc + w[0:1] * u_ref[FFN_HALO - 2:FFN_HALO - 2 + tm, :]

    def store_u(u_ref, u):
        u_ref[0:FFN_HALO] = jnp.where(i > 0, u[0:FFN_HALO], 0.0)
        u_ref[FFN_HALO:] = u[FFN_HALO:]

    for c in range(D_FF // FF_CHUNK):
        gcol = c * FF_CHUNK
        ucol = D_FF + c * FF_CHUNK
        store_u(ug_ref, jnp.dot(h, wup_ref[:, gcol:gcol + FF_CHUNK], preferred_element_type=_F32))
        store_u(uu_ref, jnp.dot(h, wup_ref[:, ucol:ucol + FF_CHUNK], preferred_element_type=_F32))
        gate = conv(ug_ref, gcol)
        up = conv(uu_ref, ucol)
        act_ref[:, gcol:gcol + FF_CHUNK] = (jax.nn.gelu(gate, approximate=True) * up).astype(_BF16)

    y = jnp.dot(act_ref[...], wdn_ref[...], preferred_element_type=_F32)
    o_ref[...] = x + y * _rms_scale(y) * post_g_ref[...]


def _ffn(x, pre_g, wup, dww, dwb, wdn, post_g):
    s = x.shape[0]
    n_tiles = s // FFN_TILE
    halo_blocks = FFN_TILE // FFN_HALO
    const = lambda i: (0, 0)
    row = lambda i: (i, 0)
    in_hbm = pl.BlockSpec(memory_space=pl.ANY)
    return pl.pallas_call(
        _ffn_kernel,
        grid=(n_tiles,),
        in_specs=[
            pl.BlockSpec((FFN_TILE, D_MODEL), row),
            pl.BlockSpec((FFN_HALO, D_MODEL), lambda i: (jnp.maximum(i * halo_blocks - 1, 0), 0)),
            pl.BlockSpec((1, D_MODEL), const),
            in_hbm,
            pl.BlockSpec((FFN_CONV_K, 2 * D_FF), const),
            pl.BlockSpec((1, 2 * D_FF), const),
            in_hbm,
            pl.BlockSpec((1, D_MODEL), const),
        ],
        out_specs=pl.BlockSpec((FFN_TILE, D_MODEL), row),
        out_shape=jax.ShapeDtypeStruct((s, D_MODEL), _F32),
        scratch_shapes=[
            pltpu.VMEM((FFN_HALO + FFN_TILE, FF_CHUNK), _F32),
            pltpu.VMEM((FFN_HALO + FFN_TILE, FF_CHUNK), _F32),
            pltpu.VMEM((FFN_TILE, D_FF), _BF16),
            pltpu.VMEM((D_MODEL, 2 * D_FF), _BF16),
            pltpu.VMEM((D_FF, D_MODEL), _BF16),
            pltpu.VMEM((STAGE_DEPTH, UP_STAGE_ROWS, 2 * D_FF), _F32),
            pltpu.VMEM((STAGE_DEPTH, FF_CHUNK, D_MODEL), _F32),
            pltpu.SemaphoreType.DMA((STAGE_DEPTH,)),
        ],
        compiler_params=pltpu.CompilerParams(
            dimension_semantics=("arbitrary",), vmem_limit_bytes=VMEM_LIMIT_BYTES),
        name="conv_ffn",
    )(x, x, pre_g, wup, dww, dwb, wdn, post_g)


def kernel(x, positions, attn_pre_g, attn_post_g, w_in, lambda_q1, lambda_k1, lambda_q2, lambda_k2,
           subln_g, conv_dw_w, conv_dw_b, conv_ln_g, conv_ln_b, w_out, ffn_pre_g, ffn_post_g, w_up,
           ffn_dw_w, ffn_dw_b, w_down):
    bsz, s, d = x.shape
    depth = w_in.shape[0]
    assert bsz == 1 and d == D_MODEL
    assert s % ROW_TILE == 0 and s % KV_TILE == 0 and s % Q_TILE == 0 and ROW_TILE == KV_TILE

    pos_row = positions.reshape(1, s).astype(jnp.int32)
    half = HEAD_DIM // 2
    inv_freq = ROPE_THETA ** (-jnp.arange(0, HEAD_DIM, 2, dtype=_F32) / HEAD_DIM)
    invf = jnp.tile(inv_freq, LANES // half).reshape(1, LANES)

    row = lambda v: v.reshape(1, -1).astype(_F32)
    xs = x[0]
    for l in range(depth):
        lambda_init = 0.8 - 0.6 * math.exp(-0.3 * l)
        q, k, vt, c = _in_proj(xs, pos_row, row(attn_pre_g[l]), invf, w_in[l].astype(_BF16))
        att = _attention(q, k, vt, pos_row, row(lambda_q1[l]), row(lambda_k1[l]),
                         row(lambda_q2[l]), row(lambda_k2[l]),
                         subln_g[l].reshape(V_DIM, 1).astype(_F32), lambda_init)
        xs = _mix(c, att, xs, conv_dw_w[l].astype(_F32), row(conv_dw_b[l]), row(conv_ln_g[l]),
                  row(conv_ln_b[l]), w_out[l].astype(_BF16), row(attn_post_g[l]))
        xs = _ffn(xs, row(ffn_pre_g[l]), w_up[l].astype(_F32), ffn_dw_w[l].astype(_F32),
                  row(ffn_dw_b[l]), w_down[l].astype(_F32), row(ffn_post_g[l]))
    return xs[None]
```
